```python
import math
import jax
import jax.numpy as jnp
from jax import lax
import numpy as np

D_MODEL = 1024
BATCH = 2
SEQ = 8192
DEPTH = 4
DEC_BATCH = 128
DEC_SEQ = 4
PAST_LEN = 8192
PAGE_SIZE = 128

N_GROUPS = 4
GROUP_W = D_MODEL // N_GROUPS
HEAD_DIM = 64
H_A = GROUP_W // HEAD_DIM
DH_A = HEAD_DIM
CHUNK_A = 128
H_B = GROUP_W // HEAD_DIM
DH_B = HEAD_DIM
H_C = GROUP_W // HEAD_DIM
DK_C = 64
DV_C = GROUP_W // H_C
CHUNK_C = 64
H_D = GROUP_W // HEAD_DIM
QK_NOPE = 64
QK_ROPE = 32
V_DIM = GROUP_W // H_D
KV_RANK = 128
ROPE_THETA = 10000.0
D_FF = 11 * D_MODEL // 4
CONV_W = 3
Q_BLOCK = 128
EPS = 1e-6
IN_SIZES = (GROUP_W, GROUP_W,
            H_B * DH_B, H_B * DH_B, H_B * DH_B, H_B,
            H_C * DK_C, H_C * DK_C, H_C * DV_C, H_C * DV_C,
            H_D * QK_NOPE, H_D * QK_ROPE, KV_RANK, QK_ROPE)
N_IN = sum(IN_SIZES)

kernel_name = 'hymba_style_hybrid_decode_step'


def rms_norm(x, g):
    xf = x.astype(jnp.float32)
    y = xf * lax.rsqrt(jnp.mean(xf * xf, axis=-1, keepdims=True) + EPS)
    return (y * g.astype(jnp.float32)).astype(x.dtype)


def split_columns(z):
    idx, acc = [], 0
    for s in IN_SIZES[:-1]:
        acc += s
        idx.append(acc)
    return jnp.split(z, idx, axis=-1)


def rope(x, pos):
    half = x.shape[-1] // 2
    inv_freq = ROPE_THETA ** (-jnp.arange(half, dtype=jnp.float32) / half)
    ang = pos.astype(jnp.float32)[:, None] * inv_freq[None, :]
    cos = jnp.cos(ang)[None, :, None, :].astype(x.dtype)
    sin = jnp.sin(ang)[None, :, None, :].astype(x.dtype)
    x1, x2 = x[..., :half], x[..., half:]
    return jnp.concatenate([x1 * cos - x2 * sin, x1 * sin + x2 * cos], axis=-1)


def sweep_query_blocks(fn, *q_arrays):
    bsz, L = q_arrays[0].shape[:2]
    if L <= Q_BLOCK or L % Q_BLOCK:
        return fn(*q_arrays)
    nb = L // Q_BLOCK
    blocks = tuple(jnp.swapaxes(a.reshape((bsz, nb, Q_BLOCK) + a.shape[2:]), 0, 1) for a in q_arrays)
    out = lax.map(lambda args: fn(*args), blocks)
    return jnp.swapaxes(out, 0, 1).reshape((bsz, L) + out.shape[3:])


def gmlp_chunk_mix(u, v, ws, bs):
    bsz, L, H, Dh = v.shape
    n = -(-L // CHUNK_A)
    pad = n * CHUNK_A - L
    vc = jnp.pad(v, ((0, 0), (0, pad), (0, 0), (0, 0))).reshape(bsz, n, CHUNK_A, H, Dh)
    w_causal = jnp.where(jnp.tril(jnp.ones((CHUNK_A, CHUNK_A), bool)), ws, 0.0).astype(v.dtype)
    mixed = jnp.einsum('hts,bnshd->bnthd', w_causal, vc) + jnp.swapaxes(bs, 0, 1)[:, :, None].astype(v.dtype)
    mixed = mixed.reshape(bsz, n * CHUNK_A, H, Dh)[:, :L]
    return u * mixed


def fox_attend(q, cq, qpos, segs):
    scale = DH_B ** -0.5
    cq_t = jnp.swapaxes(cq, 1, 2)[..., :, None]
    logits = []
    for k, _, ck, kpos in segs:
        s = jnp.einsum('bqhd,bkhd->bhqk', q, k).astype(jnp.float32) * scale
        s = s + cq_t - jnp.swapaxes(ck, 1, 2)[..., None, :]
        s = jnp.where(kpos[:, None, None, :] <= qpos[:, None, :, None], s, -jnp.inf)
        logits.append(s)
    p = jax.nn.softmax(jnp.concatenate(logits, axis=-1), axis=-1)
    out, start = 0.0, 0
    for k, v, _, _ in segs:
        n = k.shape[1]
        out = out + jnp.einsum('bhqk,bkhd->bqhd', p[..., start:start + n].astype(v.dtype), v)
        start += n
    return out


def mla_attend(q_lat, q_rope, qpos, segs):
    scale = (QK_NOPE + QK_ROPE) ** -0.5
    logits = []
    for ckv, krope, kpos in segs:
        s = (jnp.einsum('bqhr,bkr->bhqk', q_lat, ckv)
             + jnp.einsum('bqhe,bke->bhqk', q_rope, krope)).astype(jnp.float32) * scale
        s = jnp.where(kpos[:, None, None, :] <= qpos[:, None, :, None], s, -jnp.inf)
        logits.append(s)
    p = jax.nn.softmax(jnp.concatenate(logits, axis=-1), axis=-1)
    out, start = 0.0, 0
    for ckv, _, _ in segs:
        n = ckv.shape[1]
        out = out + jnp.einsum('bhqk,bkr->bqhr', p[..., start:start + n].astype(ckv.dtype), ckv)
        start += n
    return out


def hgrn2_chunked(q, k, v, log_f, s0):
    f32 = jnp.float32
    bsz, L, H, _ = q.shape
    c = math.gcd(L, CHUNK_C)
    n = L // c

    def chunks(a):
        return jnp.swapaxes(a.astype(f32).reshape((bsz, n, c) + a.shape[2:]), 0, 1)

    causal = jnp.tril(jnp.ones((c, c), bool))[None, :, :, None, None]

    def step(S, inp):
        qc, kc, vc, gc = inp
        b = jnp.cumsum(gc, axis=1)
        diff = b[:, :, None] - b[:, None, :]
        decay = jnp.where(causal, jnp.exp(jnp.where(causal, diff, 0.0)), 0.0)
        attn = jnp.einsum('bthk,btshk,bshk->bhts', qc, decay, kc)
        o = (jnp.einsum('bthk,bhkv->bthv', qc * jnp.exp(b), S)
             + jnp.einsum('bhts,bshv->bthv', attn, vc))
        b_last = b[:, -1]
        S = (jnp.exp(b_last)[..., None] * S
             + jnp.einsum('bshk,bshv->bhkv', kc * jnp.exp(b_last[:, None] - b), vc))
        return S, o

    s_fin, o = lax.scan(step, s0.astype(f32), (chunks(q), chunks(k), chunks(v), chunks(log_f)))
    return jnp.swapaxes(o, 0, 1).reshape(bsz, L, H, -1), s_fin.astype(s0.dtype)


def conv_ffn(h, w_up, conv_w, conv_b, w_down, buf):
    gate, up = jnp.split(h @ w_up, 2, axis=-1)
    L = gate.shape[1]
    ext = jnp.concatenate([buf.astype(gate.dtype), gate], axis=1)
    conv = conv_b
    for j in range(CONV_W):
        conv = conv + ext[:, j:j + L] * conv_w[j]
    y = (jax.nn.silu(conv) * up) @ w_down
    return y, ext[:, L:]


def decoder_layer(x, c, pos, lb, w, past):
    f32 = jnp.float32
    bsz, L, _ = x.shape
    mod = jax.nn.silu(c) @ w['ada_w'] + w['ada_b']
    sh1, sc1, g1, sh2, sc2, g2 = jnp.split(mod[:, None, :], 6, axis=-1)
    h = rms_norm(x, w['norm_attn_g']) * (1 + sc1) + sh1
    (a_u, a_v, b_q, b_k, b_v, b_f, c_q, c_f, c_i, c_g,
     d_qn, d_qr, d_ckv, d_kr) = split_columns(h @ w['w_in'])
    qpos = jnp.broadcast_to(pos, (bsz, L))

    a_uh = jax.nn.gelu(a_u).reshape(bsz, L, H_A, DH_A)
    a_vh = jax.nn.gelu(a_v).reshape(bsz, L, H_A, DH_A)
    out_a = gmlp_chunk_mix(a_uh, a_vh, w['gmlp_ws'], w['gmlp_b'])

    fq = b_q.reshape(bsz, L, H_B, DH_B)
    fk = b_k.reshape(bsz, L, H_B, DH_B)
    fv = b_v.reshape(bsz, L, H_B, DH_B)
    logf = jax.nn.log_sigmoid((b_f + w['fox_bf']).astype(f32))
    if past is None:
        cum = jnp.cumsum(logf, axis=1)
        segs_b = ((fk, fv, cum, qpos),)
    else:
        past_cum = jnp.cumsum(past['fox_logf'].astype(f32), axis=1)
        cum = past_cum[:, -1:] + jnp.cumsum(logf, axis=1)
        segs_b = ((past['fox_k'], past['fox_v'], past_cum, past['kpos']), (fk, fv, cum, qpos))
    out_b = sweep_query_blocks(lambda q_, cq_, qp_: fox_attend(q_, cq_, qp_, segs_b), fq, cum, qpos)

    lb_h = lb.reshape(H_C, DK_C)
    zf = c_f.astype(f32).reshape(bsz, L, H_C, DK_C)
    log_fc = jnp.logaddexp(jnp.log(lb_h), jnp.log1p(-lb_h) + jax.nn.log_sigmoid(zf))
    k_in = (1.0 - lb_h) * jax.nn.sigmoid(-zf)
    s0 = jnp.zeros((bsz, H_C, DK_C, DV_C), f32) if past is None else past['hgrn']
    o_c, s_fin = hgrn2_chunked(c_q.reshape(bsz, L, H_C, DK_C), k_in,
                               c_i.reshape(bsz, L, H_C, DV_C), log_fc, s0)
    out_c = rms_norm(o_c.astype(x.dtype), w['hgrn_norm_g']) * jax.nn.silu(c_g.reshape(bsz, L, H_C, DV_C))

    q_nope = d_qn.reshape(bsz, L, H_D, QK_NOPE)
    q_rope = rope(d_qr.reshape(bsz, L, H_D, QK_ROPE), pos)
    ckv = rms_norm(d_ckv, w['mla_kv_norm_g'])
    krope = rope(d_kr[:, :, None, :], pos)[:, :, 0, :]
    q_lat = jnp.einsum('blhd,rhd->blhr', q_nope, w['mla_w_uk'])
    if past is None:
        segs_d = ((ckv, krope, qpos),)
    else:
        segs_d = ((past['mla_ckv'], past['mla_krope'], past['kpos']), (ckv, krope, qpos))
    o_lat = sweep_query_blocks(lambda ql_, qr_, qp_: mla_attend(ql_, qr_, qp_, segs_d), q_lat, q_rope, qpos)
    out_d = jnp.einsum('blhr,rhd->blhd', o_lat, w['mla_w_uv'])

    mixed = jnp.concatenate([o.reshape(bsz, L, GROUP_W) for o in (out_a, out_b, out_c, out_d)], axis=-1)
    x = x + g1 * (mixed @ w['w_out'])

    h2 = rms_norm(x, w['norm_ffn_g']) * (1 + sc2) + sh2
    buf = jnp.zeros((bsz, CONV_W - 1, D_FF), x.dtype) if past is None else past['ffn_conv']
    y, buf_new = conv_ffn(h2, w['ffn_w_up'], w['ffn_conv_w'], w['ffn_conv_b'], w['ffn_w_down'], buf)
    x = x + g2 * y
    new = {'fox_k': fk, 'fox_v': fv, 'fox_logf': logf, 'mla_ckv': ckv, 'mla_krope': krope,
           'hgrn': s_fin, 'ffn_conv': buf_new, 'gmlp_v': a_vh}
    return x, new


def setup_inputs(seed: int = 0) -> dict:
    key = jax.random.key(seed)
    keys = iter(jax.random.split(key, 40))
    f32 = jnp.float32

    def normal(shape, scale=1.0):
        return jax.random.normal(next(keys), shape, f32) * scale

    def gain(shape):
        return 1.0 + normal(shape, 0.05)

    n_pages = PAST_LEN // PAGE_SIZE
    n_used = DEC_BATCH * n_pages
    n_phys = n_used + n_used // 4
    perm = jax.random.permutation(next(keys), n_phys)
    page_table = perm[:n_used].reshape(DEC_BATCH, n_pages).astype(jnp.int32)
    return {
        'x_prompt': normal((BATCH, SEQ, D_MODEL)),
        'x_sample': normal((DEC_BATCH, DEC_SEQ, D_MODEL)),
        'c_prompt': normal((BATCH, D_MODEL)),
        'c_sample': normal((DEC_BATCH, D_MODEL)),
        'cache_fox_k': normal((DEPTH, n_phys, PAGE_SIZE, H_B, DH_B)),
        'cache_fox_v': normal((DEPTH, n_phys, PAGE_SIZE, H_B, DH_B)),
        'cache_fox_logf': jax.nn.log_sigmoid(normal((DEPTH, n_phys, PAGE_SIZE, H_B)) + 3.0),
        'cache_mla_ckv': normal((DEPTH, n_phys, PAGE_SIZE, KV_RANK)),
        'cache_mla_krope': normal((DEPTH, n_phys, PAGE_SIZE, QK_ROPE)),
        'state_hgrn': normal((DEPTH, DEC_BATCH, H_C, DK_C, DV_C), 0.5),
        'state_ffn_conv': normal((DEPTH, DEC_BATCH, CONV_W - 1, D_FF)),
        'page_table': page_table,
        'ada_w': normal((DEPTH, D_MODEL, 6 * D_MODEL), 0.5 * D_MODEL ** -0.5),
        'ada_b': normal((DEPTH, 6 * D_MODEL), 0.02),
        'norm_attn_g': gain((DEPTH, D_MODEL)),
        'norm_ffn_g': gain((DEPTH, D_MODEL)),
        'w_in': normal((DEPTH, D_MODEL, N_IN), D_MODEL ** -0.5),
        'gmlp_ws': normal((DEPTH, H_A, CHUNK_A, CHUNK_A), CHUNK_A ** -0.5),
        'gmlp_b': 1.0 + normal((DEPTH, H_A, CHUNK_A), 0.02),
        'fox_bf': 2.0 + normal((DEPTH, H_B), 0.5),
        'hgrn_gamma': normal((DEPTH, H_C * DK_C), 0.1),
        'hgrn_norm_g': gain((DEPTH, DV_C)),
        'mla_kv_norm_g': gain((DEPTH, KV_RANK)),
        'mla_w_uk': normal((DEPTH, KV_RANK, H_D, QK_NOPE), KV_RANK ** -0.5),
        'mla_w_uv': normal((DEPTH, KV_RANK, H_D, V_DIM), KV_RANK ** -0.5),
        'w_out': normal((DEPTH, D_MODEL, D_MODEL), D_MODEL ** -0.5),
        'ffn_w_up': normal((DEPTH, D_MODEL, 2 * D_FF), D_MODEL ** -0.5),
        'ffn_conv_w': normal((DEPTH, CONV_W, D_FF), CONV_W ** -0.5),
        'ffn_conv_b': normal((DEPTH, D_FF), 0.02),
        'ffn_w_down': normal((DEPTH, D_FF, D_MODEL), D_FF ** -0.5),
        'final_norm_g': gain((D_MODEL,)),
    }


def reference(x_prompt, x_sample, c_prompt, c_sample,
              cache_fox_k, cache_fox_v, cache_fox_logf, cache_mla_ckv, cache_mla_krope,
              state_hgrn, state_ffn_conv, page_table,
              ada_w, ada_b, norm_attn_g, norm_ffn_g, w_in, gmlp_ws, gmlp_b, fox_bf,
              hgrn_gamma, hgrn_norm_g, mla_kv_norm_g, mla_w_uk, mla_w_uv, w_out,
              ffn_w_up, ffn_conv_w, ffn_conv_b, ffn_w_down, final_norm_g):
    lower_bounds = jnp.cumsum(jax.nn.softmax(hgrn_gamma.astype(jnp.float32), axis=0), axis=0)
    lower_bounds = lower_bounds - lower_bounds[0:1]
    n_dec, n_pages = page_table.shape
    past_len = n_pages * PAGE_SIZE
    pos_p = jnp.arange(x_prompt.shape[1])
    pos_s = past_len + jnp.arange(x_sample.shape[1])
    kpos_past = jnp.broadcast_to(jnp.arange(past_len), (n_dec, past_len))

    def gather(cache, l):
        g = cache[l, page_table]
        return g.reshape((n_dec, past_len) + g.shape[3:])

    hp, hs = x_prompt, x_sample
    new_p, new_s = [], []
    for l in range(DEPTH):
        w = {'ada_w': ada_w[l], 'ada_b': ada_b[l], 'norm_attn_g': norm_attn_g[l],
             'norm_ffn_g': norm_ffn_g[l], 'w_in': w_in[l], 'gmlp_ws': gmlp_ws[l], 'gmlp_b': gmlp_b[l],
             'fox_bf': fox_bf[l], 'hgrn_norm_g': hgrn_norm_g[l], 'mla_kv_norm_g': mla_kv_norm_g[l],
             'mla_w_uk': mla_w_uk[l], 'mla_w_uv': mla_w_uv[l], 'w_out': w_out[l],
             'ffn_w_up': ffn_w_up[l], 'ffn_conv_w': ffn_conv_w[l], 'ffn_conv_b': ffn_conv_b[l],
             'ffn_w_down': ffn_w_down[l]}
        hp, newp = decoder_layer(hp, c_prompt, pos_p, lower_bounds[l], w, None)
        past = {'fox_k': gather(cache_fox_k, l), 'fox_v': gather(cache_fox_v, l),
                'fox_logf': gather(cache_fox_logf, l), 'mla_ckv': gather(cache_mla_ckv, l),
                'mla_krope': gather(cache_mla_krope, l), 'hgrn': state_hgrn[l],
                'ffn_conv': state_ffn_conv[l], 'kpos': kpos_past}
        hs, news = decoder_layer(hs, c_sample, pos_s, lower_bounds[l], w, past)
        new_p.append(newp)
        new_s.append(news)

    y_prompt = rms_norm(hp, final_norm_g)
    y_sample = rms_norm(hs, final_norm_g)

    def stack(lst, name):
        return jnp.stack([d[name] for d in lst])

    def paged(a):
        return a.reshape((a.shape[0], -1, PAGE_SIZE) + a.shape[3:])

    return (y_prompt, y_sample,
            paged(stack(new_p, 'fox_k')), paged(stack(new_p, 'fox_v')), paged(stack(new_p, 'fox_logf')),
            paged(stack(new_p, 'mla_ckv')), paged(stack(new_p, 'mla_krope')),
            stack(new_p, 'hgrn'), stack(new_p, 'ffn_conv'),
            stack(new_s, 'fox_k'), stack(new_s, 'fox_v'), stack(new_s, 'fox_logf'),
            stack(new_s, 'mla_ckv'), stack(new_s, 'mla_krope'),
            stack(new_s, 'hgrn'), stack(new_s, 'ffn_conv'), stack(new_s, 'gmlp_v'))
```

```python
import functools
import math

import jax
import jax.numpy as jnp
from jax import lax
from jax.experimental import pallas as pl
from jax.experimental.pallas import tpu as pltpu

F32 = jnp.float32
BF16 = jnp.bfloat16

N_HEADS = 4
HEAD_DIM = 64
GROUP_W = N_HEADS * HEAD_DIM
CHUNK_A = 128
QK_NOPE = 64
QK_ROPE = 32
KV_RANK = 128
ROPE_THETA = 10000.0
EPS = 1e-6
CONV_W = 3
HGRN_CHUNK = 64
LANES = 128
VMEM_LIMIT = 56 * 1024 * 1024

SEG = dict(a_u=0, a_v=256, b_q=512, b_k=768, b_v=1024, c_all=1280, d_qn=2304, d_qr=2560,
           d_ckv=2688, b_f=2816, d_kr=2944)
N_IN_PACKED = 3072


def _dot(a, b):
    return jnp.dot(a, b, preferred_element_type=F32)


def _dot_nt(a, b):
    return lax.dot_general(a, b, (((1,), (1,)), ((), ())), preferred_element_type=F32)


def _dot_tn(a, b):
    return lax.dot_general(a, b, (((0,), (0,)), ((), ())), preferred_element_type=F32)


def _split3(x):
    x1 = x.astype(BF16)
    r = x - x1.astype(F32)
    x2 = r.astype(BF16)
    x3 = (r - x2.astype(F32)).astype(BF16)
    return x1, x2, x3


def _dot3_left(m, x):
    x1, x2, x3 = _split3(x)
    return _dot(m, x1) + _dot(m, x2) + _dot(m, x3)


def _dot3_right(x, m):
    x1, x2, x3 = _split3(x)
    return _dot(x1, m) + _dot(x2, m) + _dot(x3, m)


def _lane_head(shape):
    return lax.broadcasted_iota(jnp.int32, shape, len(shape) - 1) // HEAD_DIM


def _log_sigmoid(x):
    return -(jnp.maximum(-x, 0.0) + jnp.log1p(jnp.exp(-jnp.abs(x))))


def _sigmoid(x):
    return 1.0 / (1.0 + jnp.exp(-x))


def _silu(x):
    return x * _sigmoid(x)


def _gelu_tanh(x):
    return 0.5 * x * (1.0 + jnp.tanh(math.sqrt(2.0 / math.pi) * (x + 0.044715 * (x * x * x))))


def _rms(x, g):
    return x * lax.rsqrt(jnp.mean(x * x, axis=-1, keepdims=True) + EPS) * g


def _rope_lanes(x, cos, sina, sinb):
    half = QK_ROPE // 2
    return x * cos + pltpu.roll(x, half, 1) * sina + pltpu.roll(x, x.shape[1] - half, 1) * sinb


def _params(sem):
    return pltpu.CompilerParams(dimension_semantics=sem, vmem_limit_bytes=VMEM_LIMIT)


def _lb_kernel(gamma_ref, lb_ref, loglb_ref, log1m_ref):
    g = gamma_ref[...]
    e = jnp.exp(g - jnp.max(g, axis=0, keepdims=True))
    sm = e / jnp.sum(e, axis=0, keepdims=True)
    depth = g.shape[0]
    row = lax.broadcasted_iota(jnp.int32, g.shape, 0)
    cum = jnp.zeros_like(g)
    for u in range(depth):
        cum = cum + jnp.where(row >= u, sm[u:u + 1, :], 0.0)
    lb = cum - cum[0:1, :]
    lb_ref[...] = lb
    loglb_ref[...] = jnp.log(lb)
    log1m_ref[...] = jnp.log1p(-lb)


def _lower_bounds(hgrn_gamma):
    shp = jax.ShapeDtypeStruct(hgrn_gamma.shape, F32)
    return pl.pallas_call(_lb_kernel, out_shape=[shp, shp, shp], name="hgrn_lower_bounds")(hgrn_gamma)


def _ada_kernel(c_ref, w_ref, b_ref, o_ref):
    c = c_ref[...]
    o_ref[...] = _dot(_silu(c).astype(BF16), w_ref[...].astype(BF16)) + b_ref[...]


def _ada_mod(c_all, ada_w, ada_b):
    depth, d, n = ada_w.shape
    rows = c_all.shape[0]
    tn = 1536
    return pl.pallas_call(
        _ada_kernel,
        grid=(depth, n // tn),
        in_specs=[pl.BlockSpec((rows, d), lambda l, j: (0, 0)),
                  pl.BlockSpec((None, d, tn), lambda l, j: (l, 0, j)),
                  pl.BlockSpec((None, 1, tn), lambda l, j: (l, 0, j))],
        out_specs=pl.BlockSpec((None, rows, tn), lambda l, j: (l, 0, j)),
        out_shape=jax.ShapeDtypeStruct((depth, rows, n), F32),
        compiler_params=_params(("arbitrary", "arbitrary")),
        name="ada_modulation",
    )(c_all, ada_w, ada_b.reshape(depth, 1, n))


def _inproj_kernel(*refs, prompt, tiles_per_seq):
    if prompt:
        (x_ref, sh_ref, sc_ref, g_ref, w_ref, wcat_ref, bf_ref, kvg_ref, cos_ref, sina_ref, sinb_ref,
         gw_ref, gb_ref,
         oa_ref, bq_ref, bk_ref, bv_ref, logf_ref, cum_ref, call_ref, qcat_ref, kcat_ref, ckv_ref,
         krope_ref, carry_ref) = refs
    else:
        (x_ref, sh_ref, sc_ref, g_ref, w_ref, wcat_ref, bf_ref, kvg_ref, cos_ref, sina_ref, sinb_ref,
         gw_ref, gb_ref,
         oa_ref, av_ref, bq_ref, bk_ref, bv_ref, logf_ref, call_ref, qcat_ref, kcat_ref, ckv_ref,
         krope_ref, vbuf_ref) = refs
    i = pl.program_id(0)
    x = x_ref[...]
    tm = x.shape[0]
    h = _rms(x, g_ref[...]) * (1.0 + sc_ref[...]) + sh_ref[...]
    hb = h.astype(BF16)

    def seg(name, width):
        return _dot(hb, w_ref[:, SEG[name]:SEG[name] + width])

    cos, sina, sinb = cos_ref[...], sina_ref[...], sinb_ref[...]

    a_u = _gelu_tanh(seg("a_u", GROUP_W))
    a_v = _gelu_tanh(seg("a_v", GROUP_W))
    head = _lane_head((CHUNK_A if prompt else tm, GROUP_W))
    if prompt:
        r = lax.broadcasted_iota(jnp.int32, (CHUNK_A, CHUNK_A), 0)
        c = lax.broadcasted_iota(jnp.int32, (CHUNK_A, CHUNK_A), 1)
        for ch in range(tm // CHUNK_A):
            rows = slice(ch * CHUNK_A, (ch + 1) * CHUNK_A)
            vb = a_v[rows].astype(BF16)
            mixed = gb_ref[...]
            for hh in range(N_HEADS):
                wc = jnp.where(r >= c, gw_ref[hh], 0.0).astype(BF16)
                mixed = mixed + jnp.where(head == hh, _dot(wc, vb), 0.0)
            oa_ref[rows, :] = a_u[rows] * mixed
    else:
        @pl.when(i == 0)
        def _():
            vbuf_ref[...] = jnp.zeros_like(vbuf_ref)
        vbuf_ref[i] = a_v
        mixed = jnp.broadcast_to(gb_ref[...], a_v.shape)
        for s in range(vbuf_ref.shape[0]):
            mixed = mixed + gw_ref[s:s + 1, :] * vbuf_ref[s]
        oa_ref[...] = a_u * mixed
        av_ref[...] = a_v

    bq_ref[...] = (seg("b_q", GROUP_W) * (HEAD_DIM ** -0.5)).astype(BF16)
    bk_ref[...] = seg("b_k", GROUP_W)
    bv_ref[...] = seg("b_v", GROUP_W)
    lane = lax.broadcasted_iota(jnp.int32, (tm, LANES), 1)
    logf = jnp.where(lane < N_HEADS, _log_sigmoid(seg("b_f", LANES) + bf_ref[...]), 0.0)
    logf_ref[...] = logf
    if prompt:
        @pl.when(i % tiles_per_seq == 0)
        def _():
            carry_ref[...] = jnp.zeros_like(carry_ref)
        rr = lax.broadcasted_iota(jnp.int32, (tm, tm), 0)
        cc = lax.broadcasted_iota(jnp.int32, (tm, tm), 1)
        tri = jnp.where(rr >= cc, 1.0, 0.0).astype(BF16)
        cum = _dot3_left(tri, logf) + carry_ref[...]
        cum_ref[...] = cum
        carry_ref[...] = cum[tm - 1:tm, :]

    call_ref[...] = seg("c_all", 4 * GROUP_W)

    d_qn = seg("d_qn", GROUP_W)
    d_qr = _rope_lanes(seg("d_qr", LANES), cos, sina, sinb)
    qin = jnp.concatenate([d_qn, d_qr], axis=-1).astype(BF16)
    qcat_ref[...] = (_dot(qin, wcat_ref[...]) * ((QK_NOPE + QK_ROPE) ** -0.5)).astype(BF16)
    ckv = _rms(seg("d_ckv", KV_RANK), kvg_ref[...])
    krope = _rope_lanes(seg("d_kr", LANES), cos, sina, sinb)
    ckv_ref[...] = ckv
    krope_ref[...] = krope
    kcat_ref[...] = jnp.concatenate([ckv, krope], axis=-1).astype(BF16)


def _inproj(x, sh, sc, g, w, wcat, bf_pad, kvg, cos, sina, sinb, gw, gb, *, prompt, seq_len, tm):
    t, d = x.shape
    tiles = t // tm
    row = lambda width: pl.BlockSpec((tm, width), lambda i: (i, 0))
    full = lambda a: pl.BlockSpec(a.shape, lambda i: (0,) * a.ndim)
    if prompt:
        tps = seq_len // tm
        mod_spec = pl.BlockSpec((None, 1, d), lambda i: (i // tps, 0, 0))
        tab_spec = pl.BlockSpec((tm, LANES), lambda i: (i % tps, 0))
        gw_spec, gb_spec = full(gw), full(gb)
    else:
        tps = 1
        mod_spec = pl.BlockSpec((tm, d), lambda i: (0, 0))
        tab_spec = pl.BlockSpec((None, 1, LANES), lambda i: (i, 0, 0))
        gw_spec = pl.BlockSpec((None,) + gw.shape[1:], lambda i: (i, 0, 0))
        gb_spec = pl.BlockSpec((None,) + gb.shape[1:], lambda i: (i, 0, 0))
    in_specs = [row(d), mod_spec, mod_spec, full(g), full(w), full(wcat), full(bf_pad), full(kvg),
                tab_spec, tab_spec, tab_spec, gw_spec, gb_spec]
    f = lambda width, dt=F32: jax.ShapeDtypeStruct((t, width), dt)
    outs = [("oa", f(GROUP_W))]
    if not prompt:
        outs.append(("av", f(GROUP_W)))
    outs += [("bq", f(GROUP_W, BF16)), ("bk", f(GROUP_W)), ("bv", f(GROUP_W)), ("logf", f(LANES))]
    if prompt:
        outs.append(("cum", f(LANES)))
    outs += [("call", f(4 * GROUP_W)), ("qcat", f(N_HEADS * 2 * KV_RANK, BF16)), ("kcat", f(2 * KV_RANK, BF16)),
             ("ckv", f(KV_RANK)), ("krope", f(LANES))]
    scratch = [pltpu.VMEM((1, LANES), F32)] if prompt else [pltpu.VMEM((tiles, tm, GROUP_W), F32)]
    res = pl.pallas_call(
        functools.partial(_inproj_kernel, prompt=prompt, tiles_per_seq=tps),
        grid=(tiles,),
        in_specs=in_specs,
        out_specs=[row(s.shape[1]) for _, s in outs],
        out_shape=[s for _, s in outs],
        scratch_shapes=scratch,
        compiler_params=_params(("arbitrary",)),
        name="inproj_prompt" if prompt else "inproj_sample",
    )(x, sh, sc, g, w, wcat, bf_pad, kvg, cos, sina, sinb, gw, gb)
    return dict(zip([n for n, _ in outs], res))


def _online_softmax_step(s, m_ref, l_ref, acc_ref, pv):
    m_prev = m_ref[...]
    m_new = jnp.maximum(m_prev, jnp.max(s, axis=-1, keepdims=True))
    alpha = jnp.exp(m_prev - m_new)
    p = jnp.exp(s - m_new)
    l_ref[...] = alpha * l_ref[...] + jnp.sum(p, axis=-1, keepdims=True)
    acc_ref[...] = alpha * acc_ref[...] + pv(p.astype(BF16))
    m_ref[...] = m_new


def _init_softmax(m_ref, l_ref, acc_ref):
    m_ref[...] = jnp.full_like(m_ref, -jnp.inf)
    l_ref[...] = jnp.zeros_like(l_ref)
    acc_ref[...] = jnp.zeros_like(acc_ref)


def _last_kv_block(i, tq, tk):
    return ((i + 1) * tq - 1) // tk


def _fox_prefill_kernel(q_ref, k_ref, v_ref, cq_ref, ck_ref, o_ref, qst_ref, m_ref, l_ref, acc_ref, *, tq, tk):
    i, j = pl.program_id(1), pl.program_id(2)

    @pl.when(j == 0)
    def _():
        _init_softmax(m_ref, l_ref, acc_ref)
        q = q_ref[...]
        head = _lane_head(q.shape)
        for hh in range(N_HEADS):
            qst_ref[hh * tq:(hh + 1) * tq, :] = jnp.where(head == hh, q, jnp.zeros_like(q))

    @pl.when(j <= _last_kv_block(i, tq, tk))
    def _():
        s = _dot_nt(qst_ref[...], k_ref[...].astype(BF16))
        cq = cq_ref[...]
        ck = ck_ref[...]
        qpos = i * tq + lax.broadcasted_iota(jnp.int32, (tq, tk), 0)
        kpos = j * tk + lax.broadcasted_iota(jnp.int32, (tq, tk), 1)
        ok = kpos <= qpos
        parts = []
        for hh in range(N_HEADS):
            sh = s[hh * tq:(hh + 1) * tq, :] + (cq[:, hh:hh + 1] - ck[hh:hh + 1, :])
            parts.append(jnp.where(ok, sh, -jnp.inf))
        s = jnp.concatenate(parts, axis=0)
        vb = v_ref[...].astype(BF16)
        _online_softmax_step(s, m_ref, l_ref, acc_ref, lambda p: _dot(p, vb))

    @pl.when(j == pl.num_programs(2) - 1)
    def _():
        head = _lane_head((tq, GROUP_W))
        out = jnp.zeros((tq, GROUP_W), F32)
        for hh in range(N_HEADS):
            rows = slice(hh * tq, (hh + 1) * tq)
            out = out + jnp.where(head == hh, acc_ref[rows, :] / l_ref[rows, :], 0.0)
        o_ref[...] = out


def _fox_prefill(bq, bk, bv, cum, cum_t, *, batch, seq_len, tq, tk):
    nq, nk = seq_len // tq, seq_len // tk
    kv_idx = lambda b, i, j: (b * nk + jnp.minimum(j, _last_kv_block(i, tq, tk)), 0)
    return pl.pallas_call(
        functools.partial(_fox_prefill_kernel, tq=tq, tk=tk),
        grid=(batch, nq, nk),
        in_specs=[pl.BlockSpec((tq, GROUP_W), lambda b, i, j: (b * nq + i, 0)),
                  pl.BlockSpec((tk, GROUP_W), kv_idx),
                  pl.BlockSpec((tk, GROUP_W), kv_idx),
                  pl.BlockSpec((tq, LANES), lambda b, i, j: (b * nq + i, 0)),
                  pl.BlockSpec((None, N_HEADS, tk),
                               lambda b, i, j: (b, 0, jnp.minimum(j, _last_kv_block(i, tq, tk))))],
        out_specs=pl.BlockSpec((tq, GROUP_W), lambda b, i, j: (b * nq + i, 0)),
        out_shape=jax.ShapeDtypeStruct((batch * seq_len, GROUP_W), F32),
        scratch_shapes=[pltpu.VMEM((N_HEADS * tq, GROUP_W), BF16),
                        pltpu.VMEM((N_HEADS * tq, 1), F32),
                        pltpu.VMEM((N_HEADS * tq, 1), F32),
                        pltpu.VMEM((N_HEADS * tq, GROUP_W), F32)],
        compiler_params=_params(("arbitrary", "arbitrary", "arbitrary")),
        name="fox_prefill",
    )(bq, bk, bv, cum, cum_t)


def _mla_prefill_kernel(q_ref, kv_ref, wuv_ref, o_ref, qst_ref, m_ref, l_ref, acc_ref, *, tq, tk):
    i, j = pl.program_id(1), pl.program_id(2)
    dq = 2 * KV_RANK

    @pl.when(j == 0)
    def _():
        _init_softmax(m_ref, l_ref, acc_ref)
        for hh in range(N_HEADS):
            qst_ref[hh * tq:(hh + 1) * tq, :] = q_ref[:, hh * dq:(hh + 1) * dq]

    @pl.when(j <= _last_kv_block(i, tq, tk))
    def _():
        kv = kv_ref[...]
        s = _dot_nt(qst_ref[...], kv)
        qpos = i * tq + lax.broadcasted_iota(jnp.int32, (tq, tk), 0)
        kpos = j * tk + lax.broadcasted_iota(jnp.int32, (tq, tk), 1)
        ok = kpos <= qpos
        s = jnp.concatenate([jnp.where(ok, s[hh * tq:(hh + 1) * tq, :], -jnp.inf) for hh in range(N_HEADS)], axis=0)
        vb = kv[:, :KV_RANK]
        _online_softmax_step(s, m_ref, l_ref, acc_ref, lambda p: _dot(p, vb))

    @pl.when(j == pl.num_programs(2) - 1)
    def _():
        out = jnp.zeros((tq, GROUP_W), F32)
        for hh in range(N_HEADS):
            rows = slice(hh * tq, (hh + 1) * tq)
            o_lat = (acc_ref[rows, :] / l_ref[rows, :]).astype(BF16)
            out = out + _dot(o_lat, wuv_ref[hh])
        o_ref[...] = out


def _mla_prefill(qcat, kcat, wuv_pad, *, batch, seq_len, tq, tk):
    nq, nk = seq_len // tq, seq_len // tk
    return pl.pallas_call(
        functools.partial(_mla_prefill_kernel, tq=tq, tk=tk),
        grid=(batch, nq, nk),
        in_specs=[pl.BlockSpec((tq, qcat.shape[1]), lambda b, i, j: (b * nq + i, 0)),
                  pl.BlockSpec((tk, kcat.shape[1]),
                               lambda b, i, j: (b * nk + jnp.minimum(j, _last_kv_block(i, tq, tk)), 0)),
                  pl.BlockSpec(wuv_pad.shape, lambda b, i, j: (0, 0, 0))],
        out_specs=pl.BlockSpec((tq, GROUP_W), lambda b, i, j: (b * nq + i, 0)),
        out_shape=jax.ShapeDtypeStruct((batch * seq_len, GROUP_W), F32),
        scratch_shapes=[pltpu.VMEM((N_HEADS * tq, 2 * KV_RANK), BF16),
                        pltpu.VMEM((N_HEADS * tq, 1), F32),
                        pltpu.VMEM((N_HEADS * tq, 1), F32),
                        pltpu.VMEM((N_HEADS * tq, KV_RANK), F32)],
        compiler_params=_params(("arbitrary", "arbitrary", "arbitrary")),
        name="mla_prefill",
    )(qcat, kcat, wuv_pad)


def _hgrn_gates(zf, lb, loglb, log1m):
    a = loglb
    b = log1m + _log_sigmoid(zf)
    log_f = jnp.maximum(a, b) + jnp.log1p(jnp.exp(-jnp.abs(a - b)))
    k_in = (1.0 - lb) * _sigmoid(-zf)
    return log_f, k_in


def _head_rms_rows(o, gain):
    head = _lane_head(o.shape)
    sq = o * o
    ms = jnp.zeros_like(o)
    for hh in range(N_HEADS):
        ms = ms + jnp.where(head == hh, jnp.sum(jnp.where(head == hh, sq, 0.0), axis=-1, keepdims=True), 0.0)
    return o * lax.rsqrt(ms * (1.0 / HEAD_DIM) + EPS) * gain


def _hgrn_prefill_kernel(call_ref, lb_ref, loglb_ref, log1m_ref, gain_ref, o_ref, st_out_ref, st_ref, *, rows, chunk):
    i = pl.program_id(1)

    @pl.when(i == 0)
    def _():
        st_ref[...] = jnp.zeros_like(st_ref)

    w = GROUP_W
    rr = lax.broadcasted_iota(jnp.int32, (chunk, chunk), 0)
    cc = lax.broadcasted_iota(jnp.int32, (chunk, chunk), 1)
    causal = rr >= cc
    tri = jnp.where(causal, 1.0, 0.0).astype(BF16)
    head = _lane_head((chunk, w))
    bd = (lax.broadcasted_iota(jnp.int32, (w, w), 0) // HEAD_DIM) == (lax.broadcasted_iota(jnp.int32, (w, w), 1) // HEAD_DIM)
    half = chunk // 2
    for c in range(rows // chunk):
        rs = slice(c * chunk, (c + 1) * chunk)
        q = call_ref[rs, 0:w]
        zf = call_ref[rs, w:2 * w]
        v = call_ref[rs, 2 * w:3 * w]
        gate = call_ref[rs, 3 * w:4 * w]
        log_f, k_in = _hgrn_gates(zf, lb_ref[...], loglb_ref[...], log1m_ref[...])
        b = _dot3_left(tri, log_f)
        ref = b[half - 1:half, :]
        b_last = b[chunk - 1:chunk, :]
        qt = (q * jnp.exp(b - ref))
        kt = (k_in * jnp.exp(ref - b)).astype(BF16)
        vb = v.astype(BF16)
        st = st_ref[...]
        o = _dot_nt((q * jnp.exp(b)).astype(BF16), st.astype(BF16))
        for hh in range(N_HEADS):
            a = _dot_nt(jnp.where(head == hh, qt, 0.0).astype(BF16), kt)
            a = jnp.where(causal, a, 0.0).astype(BF16)
            o = o + jnp.where(head == hh, _dot(a, vb), 0.0)
        kd = (k_in * jnp.exp(b_last - b)).astype(BF16)
        st_ref[...] = st * jnp.exp(b_last) + jnp.where(bd, _dot_tn(vb, kd), 0.0)
        o_ref[rs, :] = _head_rms_rows(o, gain_ref[...]) * _silu(gate)

    st_out_ref[...] = st_ref[...]


def _hgrn_prefill(call, lb, loglb, log1m, gain, *, batch, seq_len, rows):
    tps = seq_len // rows
    vec = pl.BlockSpec((1, GROUP_W), lambda b, i: (0, 0))
    return pl.pallas_call(
        functools.partial(_hgrn_prefill_kernel, rows=rows, chunk=HGRN_CHUNK),
        grid=(batch, tps),
        in_specs=[pl.BlockSpec((rows, 4 * GROUP_W), lambda b, i: (b * tps + i, 0)), vec, vec, vec, vec],
        out_specs=[pl.BlockSpec((rows, GROUP_W), lambda b, i: (b * tps + i, 0)),
                   pl.BlockSpec((None, GROUP_W, GROUP_W), lambda b, i: (b, 0, 0))],
        out_shape=[jax.ShapeDtypeStruct((batch * seq_len, GROUP_W), F32),
                   jax.ShapeDtypeStruct((batch, GROUP_W, GROUP_W), F32)],
        scratch_shapes=[pltpu.VMEM((GROUP_W, GROUP_W), F32)],
        compiler_params=_params(("arbitrary", "arbitrary")),
        name="hgrn_prefill",
    )(call, lb, loglb, log1m, gain)


def _hgrn_decode_kernel(q_ref, zf_ref, v_ref, gate_ref, lb_ref, loglb_ref, log1m_ref, gain_ref, s_ref,
                        o_ref, s_out_ref, f_scr, k_scr, st_scr):
    n_tok = q_ref.shape[0]
    dk = s_ref.shape[0]
    log_f, k_in = _hgrn_gates(zf_ref[...], lb_ref[...], loglb_ref[...], log1m_ref[...])
    f_scr[...] = jnp.exp(log_f)
    k_scr[...] = k_in
    st_scr[...] = s_ref[...]
    for t in range(n_tok):
        v_t = v_ref[t]

        def body(k, o_acc):
            f_row = f_scr[t, pl.ds(k, 1), :]
            k_row = k_scr[t, pl.ds(k, 1), :]
            q_row = q_ref[t, pl.ds(k, 1), :]
            s_new = st_scr[k] * f_row + k_row * v_t
            st_scr[k] = s_new
            return o_acc + s_new * q_row

        o = lax.fori_loop(0, dk, body, jnp.zeros(v_t.shape, F32))
        o = o * lax.rsqrt(jnp.mean(o * o, axis=0, keepdims=True) + EPS) * gain_ref[...]
        o_ref[t] = o * _silu(gate_ref[t])
    s_out_ref[...] = st_scr[...]


def _hgrn_decode(call_t, lb_c, loglb_c, log1m_c, gain_c, state):
    n_tok, _, nb = call_t.shape
    nh, dk, dv, _ = state.shape
    part = lambda p: pl.BlockSpec((n_tok, HEAD_DIM, nb), lambda h: (0, p * N_HEADS + h, 0))
    col = pl.BlockSpec((HEAD_DIM, 1), lambda h: (h, 0))
    return pl.pallas_call(
        _hgrn_decode_kernel,
        grid=(nh,),
        in_specs=[part(0), part(1), part(2), part(3), col, col, col,
                  pl.BlockSpec((HEAD_DIM, 1), lambda h: (0, 0)),
                  pl.BlockSpec((None, dk, dv, nb), lambda h: (h, 0, 0, 0))],
        out_specs=[pl.BlockSpec((n_tok, HEAD_DIM, nb), lambda h: (0, h, 0)),
                   pl.BlockSpec((None, dk, dv, nb), lambda h: (h, 0, 0, 0))],
        out_shape=[jax.ShapeDtypeStruct((n_tok, GROUP_W, nb), F32),
                   jax.ShapeDtypeStruct(state.shape, F32)],
        scratch_shapes=[pltpu.VMEM((n_tok, HEAD_DIM, nb), F32),
                        pltpu.VMEM((n_tok, HEAD_DIM, nb), F32),
                        pltpu.VMEM((dk, dv, nb), F32)],
        compiler_params=_params(("arbitrary",)),
        name="hgrn_decode",
    )(call_t, call_t, call_t, call_t, lb_c, loglb_c, log1m_c, gain_c, state)


Q_PAD = 8


def _fox_decode_kernel(pt_ref, q_ref, kn_ref, vn_ref, lfn_ref, lfnt_ref, *rest, pps):
    k_refs = rest[0:pps]
    v_refs = rest[pps:2 * pps]
    lf_refs = rest[2 * pps:3 * pps]
    o_ref = rest[3 * pps]
    qbd_ref, kn_pad, vn_pad, m_ref, l_ref, acc_ref, carry_ref, erow_ref = rest[3 * pps + 1:]
    b, c = pl.program_id(0), pl.program_id(1)
    page = kn_pad.shape[0]
    nrow = N_HEADS * Q_PAD
    su = lax.broadcasted_iota(jnp.int32, (page, page), 0)
    sv = lax.broadcasted_iota(jnp.int32, (page, page), 1)
    later = jnp.where(su > sv, 1.0, 0.0).astype(BF16)

    @pl.when((b == 0) & (c == 0))
    def _():
        kn_pad[...] = jnp.zeros_like(kn_pad)
        vn_pad[...] = jnp.zeros_like(vn_pad)

    @pl.when(c == 0)
    def _():
        _init_softmax(m_ref, l_ref, acc_ref)
        q = q_ref[...].astype(F32)
        head = _lane_head(q.shape)
        for hh in range(N_HEADS):
            qbd_ref[hh * Q_PAD:(hh + 1) * Q_PAD, :] = jnp.where(head == hh, q, 0.0)
        kn_pad[0:Q_PAD, :] = kn_ref[...]
        vn_pad[0:Q_PAD, :] = vn_ref[...]
        lfn = lfn_ref[...]
        trow = lax.broadcasted_iota(jnp.int32, lfn.shape, 0)
        e_q = jnp.zeros_like(lfn)
        for u in range(1, Q_PAD):
            e_q = e_q + jnp.where(trow < u, lfn[u:u + 1, :], 0.0)
        for hh in range(N_HEADS):
            erow_ref[hh * Q_PAD:(hh + 1) * Q_PAD, :] = e_q[:, hh:hh + 1]
        lfnt = lfnt_ref[...]
        e_k = _dot3_right(lfnt, later)
        carry_ref[...] = jnp.sum(lfnt, axis=-1, keepdims=True)
        bias = jnp.concatenate([jnp.broadcast_to(e_k[hh:hh + 1, :], (Q_PAD, page)) for hh in range(N_HEADS)], axis=0)
        s = _dot_nt(qbd_ref[...].astype(BF16), kn_pad[...].astype(BF16)) + (bias - erow_ref[...])
        col = lax.broadcasted_iota(jnp.int32, (nrow, page), 1)
        qtok = lax.broadcasted_iota(jnp.int32, (nrow, page), 0) % Q_PAD
        s = jnp.where(col <= qtok, s, -jnp.inf)
        _online_softmax_step(s, m_ref, l_ref, acc_ref, lambda p: _dot(p, vn_pad[...].astype(BF16)))

    qbd = qbd_ref[...].astype(BF16)
    erow = erow_ref[...]
    carry = carry_ref[...]
    parts = []
    for jj in range(pps - 1, -1, -1):
        lf = lf_refs[jj][...]
        e_k = _dot3_right(lf, later) + carry
        carry = carry + jnp.sum(lf, axis=-1, keepdims=True)
        bias = jnp.concatenate([jnp.broadcast_to(e_k[hh:hh + 1, :], (Q_PAD, page)) for hh in range(N_HEADS)], axis=0)
        kt = k_refs[jj][...].reshape(GROUP_W, page).astype(BF16)
        parts.append(_dot(qbd, kt) + (bias - erow))
    carry_ref[...] = carry
    s = jnp.concatenate(parts, axis=-1)

    def pv(p):
        out = jnp.zeros((nrow, GROUP_W), F32)
        for n, jj in enumerate(range(pps - 1, -1, -1)):
            vt = v_refs[jj][...].reshape(GROUP_W, page).astype(BF16)
            out = out + _dot_nt(p[:, n * page:(n + 1) * page], vt)
        return out

    _online_softmax_step(s, m_ref, l_ref, acc_ref, pv)

    @pl.when(c == pl.num_programs(1) - 1)
    def _():
        head = _lane_head((Q_PAD, GROUP_W))
        out = jnp.zeros((Q_PAD, GROUP_W), F32)
        for hh in range(N_HEADS):
            rows = slice(hh * Q_PAD, (hh + 1) * Q_PAD)
            out = out + jnp.where(head == hh, acc_ref[rows, :] / l_ref[rows, :], 0.0)
        o_ref[...] = out


def _fox_decode(page_table, q, kn, vn, lfn, lfnt, cache_k, cache_v, cache_lf, *, layer, pps):
    nb, n_pages = page_table.shape
    page = cache_k.shape[-1]
    n_chunks = n_pages // pps
    seq = lambda a: pl.BlockSpec((None,) + a.shape[1:], lambda b, c, pt: (b,) + (0,) * (a.ndim - 1))

    def paged(a, jj):
        def idx(b, c, pt):
            return (layer, pt[b, (n_chunks - 1 - c) * pps + jj]) + (0,) * (a.ndim - 2)
        return pl.BlockSpec((None, None) + a.shape[2:], idx)

    in_specs = [seq(q), seq(kn), seq(vn), seq(lfn), seq(lfnt)]
    in_specs += [paged(cache_k, jj) for jj in range(pps)]
    in_specs += [paged(cache_v, jj) for jj in range(pps)]
    in_specs += [paged(cache_lf, jj) for jj in range(pps)]
    nrow = N_HEADS * Q_PAD
    return pl.pallas_call(
        functools.partial(_fox_decode_kernel, pps=pps),
        grid_spec=pltpu.PrefetchScalarGridSpec(
            num_scalar_prefetch=1,
            grid=(nb, n_chunks),
            in_specs=in_specs,
            out_specs=pl.BlockSpec((None, Q_PAD, GROUP_W), lambda b, c, pt: (b, 0, 0)),
            scratch_shapes=[pltpu.VMEM((nrow, GROUP_W), F32),
                            pltpu.VMEM((page, GROUP_W), F32),
                            pltpu.VMEM((page, GROUP_W), F32),
                            pltpu.VMEM((nrow, 1), F32),
                            pltpu.VMEM((nrow, 1), F32),
                            pltpu.VMEM((nrow, GROUP_W), F32),
                            pltpu.VMEM((N_HEADS, 1), F32),
                            pltpu.VMEM((nrow, 1), F32)]),
        out_shape=jax.ShapeDtypeStruct((nb, Q_PAD, GROUP_W), F32),
        compiler_params=_params(("arbitrary", "arbitrary")),
        name="fox_decode",
    )(page_table, q, kn, vn, lfn, lfnt, *([cache_k] * pps), *([cache_v] * pps), *([cache_lf] * pps))


def _mla_decode_kernel(pt_ref, q_ref, kn_ref, wuv_ref, *rest, pps):
    ckv_refs = rest[0:pps]
    kr_refs = rest[pps:2 * pps]
    o_ref = rest[2 * pps]
    qst_ref, kn_pad, m_ref, l_ref, acc_ref = rest[2 * pps + 1:]
    b, c = pl.program_id(0), pl.program_id(1)
    page = kn_pad.shape[0]
    nrow = N_HEADS * Q_PAD
    dq = 2 * KV_RANK

    @pl.when((b == 0) & (c == 0))
    def _():
        kn_pad[...] = jnp.zeros_like(kn_pad)

    @pl.when(c == 0)
    def _():
        _init_softmax(m_ref, l_ref, acc_ref)
        for hh in range(N_HEADS):
            qst_ref[hh * Q_PAD:(hh + 1) * Q_PAD, :] = q_ref[:, hh * dq:(hh + 1) * dq].astype(F32)
        kn_pad[0:Q_PAD, :] = kn_ref[...].astype(F32)
        s = _dot_nt(qst_ref[...].astype(BF16), kn_pad[...].astype(BF16))
        col = lax.broadcasted_iota(jnp.int32, (nrow, page), 1)
        qtok = lax.broadcasted_iota(jnp.int32, (nrow, page), 0) % Q_PAD
        s = jnp.where(col <= qtok, s, -jnp.inf)
        _online_softmax_step(s, m_ref, l_ref, acc_ref, lambda p: _dot(p, kn_pad[:, :KV_RANK].astype(BF16)))

    qst = qst_ref[...].astype(BF16)
    q_lat = qst[:, :KV_RANK]
    q_rope = qst[:, KV_RANK:]
    n_rope = kr_refs[0].shape[0]
    zpad = jnp.zeros((KV_RANK - n_rope, page), BF16)
    parts = []
    for jj in range(pps):
        ckv = ckv_refs[jj][...].astype(BF16)
        krt = jnp.concatenate([kr_refs[jj][...].astype(BF16), zpad], axis=0)
        parts.append(_dot_nt(q_lat, ckv) + _dot(q_rope, krt))
    s = jnp.concatenate(parts, axis=-1)

    def pv(p):
        out = jnp.zeros((nrow, KV_RANK), F32)
        for jj in range(pps):
            out = out + _dot(p[:, jj * page:(jj + 1) * page], ckv_refs[jj][...].astype(BF16))
        return out

    _online_softmax_step(s, m_ref, l_ref, acc_ref, pv)

    @pl.when(c == pl.num_programs(1) - 1)
    def _():
        out = jnp.zeros((Q_PAD, GROUP_W), F32)
        for hh in range(N_HEADS):
            rows = slice(hh * Q_PAD, (hh + 1) * Q_PAD)
            o_lat = (acc_ref[rows, :] / l_ref[rows, :]).astype(BF16)
            out = out + _dot(o_lat, wuv_ref[hh])
        o_ref[...] = out


def _mla_decode(page_table, q, kn, wuv_pad, cache_ckv, cache_krt, *, layer, pps):
    nb, n_pages = page_table.shape
    page = cache_ckv.shape[2]
    n_chunks = n_pages // pps
    seq = lambda a: pl.BlockSpec((None,) + a.shape[1:], lambda b, c, pt: (b,) + (0,) * (a.ndim - 1))

    def paged(a, jj):
        def idx(b, c, pt):
            return (layer, pt[b, c * pps + jj], 0, 0)
        return pl.BlockSpec((None, None) + a.shape[2:], idx)

    in_specs = [seq(q), seq(kn), pl.BlockSpec(wuv_pad.shape, lambda b, c, pt: (0, 0, 0))]
    in_specs += [paged(cache_ckv, jj) for jj in range(pps)]
    in_specs += [paged(cache_krt, jj) for jj in range(pps)]
    nrow = N_HEADS * Q_PAD
    return pl.pallas_call(
        functools.partial(_mla_decode_kernel, pps=pps),
        grid_spec=pltpu.PrefetchScalarGridSpec(
            num_scalar_prefetch=1,
            grid=(nb, n_chunks),
            in_specs=in_specs,
            out_specs=pl.BlockSpec((None, Q_PAD, GROUP_W), lambda b, c, pt: (b, 0, 0)),
            scratch_shapes=[pltpu.VMEM((nrow, 2 * KV_RANK), F32),
                            pltpu.VMEM((page, 2 * KV_RANK), F32),
                            pltpu.VMEM((nrow, 1), F32),
                            pltpu.VMEM((nrow, 1), F32),
                            pltpu.VMEM((nrow, KV_RANK), F32)]),
        out_shape=jax.ShapeDtypeStruct((nb, Q_PAD, GROUP_W), F32),
        compiler_params=_params(("arbitrary", "arbitrary")),
        name="mla_decode",
    )(page_table, q, kn, wuv_pad, *([cache_ckv] * pps), *([cache_krt] * pps))


FF_CHUNK = 1408
CARRY_ROWS = 8


def _mixffn_kernel(*refs, prompt, tiles_per_seq, final):
    if prompt:
        (x_ref, oa_ref, ob_ref, oc_ref, od_ref, g1_ref, sh2_ref, sc2_ref, g2_ref, ng_ref, wo_ref, wup_ref,
         cw_ref, cb_ref, wdn_ref, fg_ref, y_ref, nbuf_ref, ext_ref) = refs
    else:
        (x_ref, oa_ref, ob_ref, oc_ref, od_ref, g1_ref, sh2_ref, sc2_ref, g2_ref, ng_ref, wo_ref, wup_ref,
         cw_ref, cb_ref, wdn_ref, fg_ref, buf_ref, y_ref, gate_ref, hist_ref) = refs
    i = pl.program_id(0)
    x = x_ref[...]
    tm, d = x.shape
    d_ff = wdn_ref.shape[0]
    mix = jnp.zeros((tm, d), F32)
    for n, r in enumerate((oa_ref, ob_ref, oc_ref, od_ref)):
        mix = mix + _dot(r[...].astype(BF16), wo_ref[n * GROUP_W:(n + 1) * GROUP_W, :])
    x1 = x + g1_ref[...] * mix
    h2 = (_rms(x1, ng_ref[...]) * (1.0 + sc2_ref[...]) + sh2_ref[...]).astype(BF16)

    if prompt:
        @pl.when(i % tiles_per_seq == 0)
        def _():
            ext_ref[0:CARRY_ROWS, :] = jnp.zeros((CARRY_ROWS, d_ff), F32)
    else:
        @pl.when(i == 0)
        def _():
            hist_ref[...] = buf_ref[...]

    y = jnp.zeros((tm, d), F32)
    for c0 in range(0, d_ff, FF_CHUNK):
        cols = slice(c0, c0 + FF_CHUNK)
        gate = _dot(h2, wup_ref[:, cols])
        up = _dot(h2, wup_ref[:, d_ff + c0:d_ff + c0 + FF_CHUNK])
        if prompt:
            ext_ref[CARRY_ROWS:CARRY_ROWS + tm, cols] = gate
            prev1 = ext_ref[pl.ds(CARRY_ROWS - 1, tm), cols]
            prev2 = ext_ref[pl.ds(CARRY_ROWS - 2, tm), cols]
        else:
            prev2 = hist_ref[0, :, cols]
            prev1 = hist_ref[1, :, cols]
            hist_ref[0, :, cols] = prev1
            hist_ref[1, :, cols] = gate
            gate_ref[:, cols] = gate
        conv = cb_ref[:, cols] + prev2 * cw_ref[0:1, cols] + prev1 * cw_ref[1:2, cols] + gate * cw_ref[2:3, cols]
        act = (_silu(conv) * up).astype(BF16)
        y = y + _dot(act, wdn_ref[cols, :])
    if prompt:
        tail = ext_ref[tm:tm + CARRY_ROWS, :]
        ext_ref[0:CARRY_ROWS, :] = tail
        nbuf_ref[...] = tail[CARRY_ROWS - (CONV_W - 1):, :]
    x2 = x1 + g2_ref[...] * y
    if final:
        x2 = _rms(x2, fg_ref[...])
    y_ref[...] = x2


def _mixffn(x, oa, ob, oc, od, g1, sh2, sc2, g2, ng, wo, wup, cw, cb, wdn, fg, buf, *, prompt, seq_len, tm, final):
    t, d = x.shape
    d_ff = wdn.shape[0]
    tiles = t // tm
    row = lambda width: pl.BlockSpec((tm, width), lambda i: (i, 0))
    full = lambda a: pl.BlockSpec(a.shape, lambda i: (0,) * a.ndim)
    if prompt:
        tps = seq_len // tm
        mod_spec = pl.BlockSpec((None, 1, d), lambda i: (i // tps, 0, 0))
    else:
        tps = 1
        mod_spec = pl.BlockSpec((tm, d), lambda i: (0, 0))
    in_specs = [row(d), row(GROUP_W), row(GROUP_W), row(GROUP_W), row(GROUP_W), mod_spec, mod_spec, mod_spec,
                mod_spec, full(ng), full(wo), full(wup), full(cw), full(cb), full(wdn), full(fg)]
    args = [x, oa, ob, oc, od, g1, sh2, sc2, g2, ng, wo, wup, cw, cb, wdn, fg]
    if prompt:
        out_specs = [row(d), pl.BlockSpec((None, CONV_W - 1, d_ff), lambda i: (i // tps, 0, 0))]
        out_shape = [jax.ShapeDtypeStruct((t, d), F32), jax.ShapeDtypeStruct((t // seq_len, CONV_W - 1, d_ff), F32)]
        scratch = [pltpu.VMEM((CARRY_ROWS + tm, d_ff), F32)]
    else:
        in_specs.append(full(buf))
        args.append(buf)
        out_specs = [row(d), row(d_ff)]
        out_shape = [jax.ShapeDtypeStruct((t, d), F32), jax.ShapeDtypeStruct((t, d_ff), F32)]
        scratch = [pltpu.VMEM((CONV_W - 1, tm, d_ff), F32)]
    return pl.pallas_call(
        functools.partial(_mixffn_kernel, prompt=prompt, tiles_per_seq=tps, final=final),
        grid=(tiles,),
        in_specs=in_specs,
        out_specs=out_specs,
        out_shape=out_shape,
        scratch_shapes=scratch,
        compiler_params=_params(("arbitrary",)),
        name="mixffn_prompt" if prompt else "mixffn_sample",
    )(*args)


def _rope_tables(pos):
    half = QK_ROPE // 2
    inv_freq = ROPE_THETA ** (-jnp.arange(half, dtype=F32) / half)
    ang = pos.astype(F32)[:, None] * inv_freq[None, :]
    cos, sin, zero = jnp.cos(ang), jnp.sin(ang), jnp.zeros_like(ang)
    reps = LANES // QK_ROPE
    tile = lambda lo, hi: jnp.tile(jnp.concatenate([lo, hi], axis=-1), (1, reps))
    return tile(cos, cos), tile(zero, sin), tile(-sin, zero)


def _pack_w_in(w_in):
    sizes = (256, 256, 256, 256, 256, 4, 256, 256, 256, 256, 256, 128, 128, 32)
    offs = [0]
    for s in sizes:
        offs.append(offs[-1] + s)
    col = lambda n: w_in[:, :, offs[n]:offs[n + 1]]
    pad = lambda a: jnp.pad(a, ((0, 0), (0, 0), (0, LANES - a.shape[2])))
    parts = [col(0), col(1), col(2), col(3), col(4), col(6), col(7), col(8), col(9), col(10), col(11), col(12),
             pad(col(5)), pad(col(13))]
    return jnp.concatenate(parts, axis=-1).astype(BF16)


def _pack_w_qcat(w_uk):
    r = w_uk.shape[0]
    w = jnp.zeros((N_HEADS * QK_NOPE + N_HEADS * QK_ROPE, N_HEADS * 2 * r), F32)
    eye = jnp.eye(QK_ROPE, dtype=F32)
    for hh in range(N_HEADS):
        w = w.at[hh * QK_NOPE:(hh + 1) * QK_NOPE, hh * 2 * r:hh * 2 * r + r].set(w_uk[:, hh, :].T)
        w = w.at[N_HEADS * QK_NOPE + hh * QK_ROPE:N_HEADS * QK_NOPE + (hh + 1) * QK_ROPE,
                 hh * 2 * r + r:hh * 2 * r + r + QK_ROPE].set(eye)
    return w.astype(BF16)


def _pack_w_uv(w_uv):
    r = w_uv.shape[0]
    w = jnp.zeros((N_HEADS, r, GROUP_W), F32)
    for hh in range(N_HEADS):
        w = w.at[hh, :, hh * HEAD_DIM:(hh + 1) * HEAD_DIM].set(w_uv[:, hh, :])
    return w.astype(BF16)


def _pick(n, candidates):
    for c in candidates:
        if n % c == 0:
            return c
    return n


def kernel(x_prompt, x_sample, c_prompt, c_sample, cache_fox_k, cache_fox_v, cache_fox_logf, cache_mla_ckv, cache_mla_krope, state_hgrn, state_ffn_conv, page_table, ada_w, ada_b, norm_attn_g, norm_ffn_g, w_in, gmlp_ws, gmlp_b, fox_bf, hgrn_gamma, hgrn_norm_g, mla_kv_norm_g, mla_w_uk, mla_w_uv, w_out, ffn_w_up, ffn_conv_w, ffn_conv_b, ffn_w_down, final_norm_g):
    depth = w_in.shape[0]
    bp, seq_len, d = x_prompt.shape
    nb, n_tok, _ = x_sample.shape
    n_pages = page_table.shape[1]
    page = cache_fox_k.shape[2]
    d_ff = ffn_w_down.shape[1]
    past_len = n_pages * page
    tp = bp * seq_len

    tm_p = _pick(seq_len, (256, 128))
    tq = _pick(seq_len, (256, 128))
    tk = _pick(seq_len, (512, 256, 128))
    hg_rows = _pick(seq_len, (256, 128, 64))
    pps = _pick(n_pages, (16, 8, 4, 2, 1))

    w_in_p = _pack_w_in(w_in)
    w_out_b = w_out.astype(BF16)
    w_up_b = ffn_w_up.astype(BF16)
    w_dn_b = ffn_w_down.astype(BF16)
    bf_pad = jnp.pad(fox_bf, ((0, 0), (0, LANES - fox_bf.shape[1])))
    gain_row = jnp.tile(hgrn_norm_g, (1, N_HEADS))
    lb, loglb, log1m = _lower_bounds(hgrn_gamma.astype(F32))
    cos_p, sina_p, sinb_p = _rope_tables(jnp.arange(seq_len))
    cos_s, sina_s, sinb_s = (a.reshape(n_tok, 1, LANES) for a in _rope_tables(past_len + jnp.arange(n_tok)))
    gb_p = jnp.repeat(jnp.swapaxes(gmlp_b, 1, 2), HEAD_DIM, axis=2)
    gw_s = jnp.repeat(jnp.transpose(gmlp_ws[:, :, :n_tok, :n_tok], (0, 2, 3, 1)), HEAD_DIM, axis=3)
    gw_s = jnp.where((jnp.arange(n_tok)[:, None] >= jnp.arange(n_tok)[None, :])[None, :, :, None], gw_s, 0.0)
    gb_s = gb_p[:, :n_tok].reshape(depth, n_tok, 1, GROUP_W)

    rows_c = bp + nb
    rows_pad = -(-rows_c // 8) * 8
    c_all = jnp.pad(jnp.concatenate([c_prompt, c_sample], axis=0), ((0, rows_pad - rows_c), (0, 0)))
    mod = _ada_mod(c_all, ada_w, ada_b)

    ck_view = jnp.transpose(cache_fox_k, (0, 1, 3, 4, 2))
    cv_view = jnp.transpose(cache_fox_v, (0, 1, 3, 4, 2))
    clf_view = jnp.transpose(cache_fox_logf, (0, 1, 3, 2))
    ckr_view = jnp.transpose(cache_mla_krope, (0, 1, 3, 2))
    st_view = jnp.transpose(state_hgrn, (0, 2, 3, 4, 1))

    hp = x_prompt.reshape(tp, d)
    hs = jnp.swapaxes(x_sample, 0, 1).reshape(n_tok * nb, d)
    new_p = {k: [] for k in ("fox_k", "fox_v", "fox_logf", "mla_ckv", "mla_krope", "hgrn", "ffn_conv")}
    new_s = {k: [] for k in ("fox_k", "fox_v", "fox_logf", "mla_ckv", "mla_krope", "hgrn", "ffn_conv", "gmlp_v")}

    def seq_major(a):
        return jnp.swapaxes(a.reshape(n_tok, nb, a.shape[-1]), 0, 1)

    def pad_q(a):
        return jnp.pad(a, ((0, 0), (0, Q_PAD - n_tok), (0, 0)))

    for l in range(depth):
        m6 = mod[l].reshape(rows_pad, 6, d)
        modp = [m6[:bp, n].reshape(bp, 1, d) for n in range(6)]
        mods = [m6[bp:bp + nb, n] for n in range(6)]
        wcat = _pack_w_qcat(mla_w_uk[l])
        wuv_pad = _pack_w_uv(mla_w_uv[l])
        vec = lambda a: a[l].reshape(1, -1)
        final = l == depth - 1

        zp = _inproj(hp, modp[0], modp[1], vec(norm_attn_g), w_in_p[l], wcat, vec(bf_pad), vec(mla_kv_norm_g),
                     cos_p, sina_p, sinb_p, gmlp_ws[l], gb_p[l], prompt=True, seq_len=seq_len, tm=tm_p)
        cum_t = jnp.swapaxes(zp["cum"][:, :N_HEADS].reshape(bp, seq_len, N_HEADS), 1, 2)
        ob = _fox_prefill(zp["bq"], zp["bk"], zp["bv"], zp["cum"], cum_t, batch=bp, seq_len=seq_len, tq=tq, tk=tk)
        oc, st_p = _hgrn_prefill(zp["call"], vec(lb), vec(loglb), vec(log1m), vec(gain_row),
                                 batch=bp, seq_len=seq_len, rows=hg_rows)
        od = _mla_prefill(zp["qcat"], zp["kcat"], wuv_pad, batch=bp, seq_len=seq_len, tq=tq, tk=tk)
        hp, nbuf_p = _mixffn(hp, zp["oa"], ob, oc, od, modp[2], modp[3], modp[4], modp[5], vec(norm_ffn_g),
                             w_out_b[l], w_up_b[l], ffn_conv_w[l], vec(ffn_conv_b), w_dn_b[l],
                             final_norm_g.reshape(1, d), None, prompt=True, seq_len=seq_len, tm=tm_p, final=final)
        new_p["fox_k"].append(zp["bk"])
        new_p["fox_v"].append(zp["bv"])
        new_p["fox_logf"].append(zp["logf"][:, :N_HEADS])
        new_p["mla_ckv"].append(zp["ckv"])
        new_p["mla_krope"].append(zp["krope"][:, :QK_ROPE])
        st4 = st_p.reshape(bp, N_HEADS, HEAD_DIM, N_HEADS, HEAD_DIM)
        new_p["hgrn"].append(jnp.stack([jnp.swapaxes(st4[:, hh, :, hh, :], 1, 2) for hh in range(N_HEADS)], axis=1))
        new_p["ffn_conv"].append(nbuf_p)

        zs = _inproj(hs, mods[0], mods[1], vec(norm_attn_g), w_in_p[l], wcat, vec(bf_pad), vec(mla_kv_norm_g),
                     cos_s, sina_s, sinb_s, gw_s[l], gb_s[l], prompt=False, seq_len=n_tok, tm=nb)
        q_s, kn_s, vn_s = (pad_q(seq_major(zs[n])) for n in ("bq", "bk", "bv"))
        lf_s = seq_major(zs["logf"])
        lfn = pad_q(lf_s)
        lfnt = jnp.pad(jnp.swapaxes(lf_s[:, :, :N_HEADS], 1, 2), ((0, 0), (0, 0), (0, page - n_tok)))
        ob_s = _fox_decode(page_table, q_s, kn_s, vn_s, lfn, lfnt, ck_view, cv_view, clf_view, layer=l, pps=pps)
        od_s = _mla_decode(page_table, pad_q(seq_major(zs["qcat"])), pad_q(seq_major(zs["kcat"])), wuv_pad,
                           cache_mla_ckv, ckr_view, layer=l, pps=pps)
        to_rows = lambda a: jnp.swapaxes(a[:, :n_tok], 0, 1).reshape(n_tok * nb, GROUP_W)
        call_t = jnp.swapaxes(zs["call"].reshape(n_tok, nb, 4 * GROUP_W), 1, 2)
        colv = lambda a: a[l].reshape(-1, 1)
        oc_t, st_s = _hgrn_decode(call_t, colv(lb), colv(loglb), colv(log1m), hgrn_norm_g[l].reshape(-1, 1), st_view[l])
        oc_s = jnp.swapaxes(oc_t, 1, 2).reshape(n_tok * nb, GROUP_W)
        buf_t = jnp.swapaxes(state_ffn_conv[l], 0, 1)
        hs, gate_s = _mixffn(hs, zs["oa"], to_rows(ob_s), oc_s, to_rows(od_s), mods[2], mods[3], mods[4], mods[5],
                             vec(norm_ffn_g), w_out_b[l], w_up_b[l], ffn_conv_w[l], vec(ffn_conv_b), w_dn_b[l],
                             final_norm_g.reshape(1, d), buf_t, prompt=False, seq_len=n_tok, tm=nb, final=final)
        new_s["fox_k"].append(seq_major(zs["bk"]))
        new_s["fox_v"].append(seq_major(zs["bv"]))
        new_s["fox_logf"].append(lf_s[:, :, :N_HEADS])
        new_s["mla_ckv"].append(seq_major(zs["ckv"]))
        new_s["mla_krope"].append(seq_major(zs["krope"])[:, :, :QK_ROPE])
        new_s["hgrn"].append(jnp.transpose(st_s, (3, 0, 1, 2)))
        new_s["ffn_conv"].append(seq_major(gate_s)[:, n_tok - (CONV_W - 1):])
        new_s["gmlp_v"].append(seq_major(zs["av"]))

    y_prompt = hp.reshape(bp, seq_len, d)
    y_sample = seq_major(hs)
    n_pp = tp // page
    stk = lambda lst: jnp.stack(lst)
    return (y_prompt, y_sample,
            stk(new_p["fox_k"]).reshape(depth, n_pp, page, N_HEADS, HEAD_DIM),
            stk(new_p["fox_v"]).reshape(depth, n_pp, page, N_HEADS, HEAD_DIM),
            stk(new_p["fox_logf"]).reshape(depth, n_pp, page, N_HEADS),
            stk(new_p["mla_ckv"]).reshape(depth, n_pp, page, KV_RANK),
            stk(new_p["mla_krope"]).reshape(depth, n_pp, page, QK_ROPE),
            stk(new_p["hgrn"]), stk(new_p["ffn_conv"]),
            stk(new_s["fox_k"]).reshape(depth, nb, n_tok, N_HEADS, HEAD_DIM),
            stk(new_s["fox_v"]).reshape(depth, nb, n_tok, N_HEADS, HEAD_DIM),
            stk(new_s["fox_logf"]),
            stk(new_s["mla_ckv"]), stk(new_s["mla_krope"]),
            stk(new_s["hgrn"]), stk(new_s["ffn_conv"]),
            stk(new_s["gmlp_v"]).reshape(depth, nb, n_tok, N_HEADS, HEAD_DIM))
```

```python
import functools
import math

import jax
import jax.numpy as jnp
from jax import lax
from jax.experimental import pallas as pl
from jax.experimental.pallas import tpu as pltpu

F32 = jnp.float32
BF16 = jnp.bfloat16

N_HEADS = 4
HEAD_DIM = 64
GROUP_W = N_HEADS * HEAD_DIM
CHUNK_A = 128
QK_NOPE = 64
QK_ROPE = 32
KV_RANK = 128
ROPE_THETA = 10000.0
EPS = 1e-6
CONV_W = 3
HGRN_CHUNK = 64
LANES = 128
VMEM_LIMIT = 56 * 1024 * 1024

SEG = dict(a_u=0, a_v=256, b_q=512, b_k=768, b_v=1024, c_all=1280, d_qn=2304, d_qr=2560,
           d_ckv=2688, b_f=2816, d_kr=2944)
N_IN_PACKED = 3072


def _dot(a, b):
    return jnp.dot(a, b, preferred_element_type=F32)


def _dot_nt(a, b):
    return lax.dot_general(a, b, (((1,), (1,)), ((), ())), preferred_element_type=F32)


def _dot_tn(a, b):
    return lax.dot_general(a, b, (((0,), (0,)), ((), ())), preferred_element_type=F32)


def _split3(x):
    x1 = x.astype(BF16)
    r = x - x1.astype(F32)
    x2 = r.astype(BF16)
    x3 = (r - x2.astype(F32)).astype(BF16)
    return x1, x2, x3


def _dot3_left(m, x):
    x1, x2, x3 = _split3(x)
    return _dot(m, x1) + _dot(m, x2) + _dot(m, x3)


def _dot3_right(x, m):
    x1, x2, x3 = _split3(x)
    return _dot(x1, m) + _dot(x2, m) + _dot(x3, m)


def _lane_head(shape):
    return lax.broadcasted_iota(jnp.int32, shape, len(shape) - 1) // HEAD_DIM


def _log_sigmoid(x):
    return -(jnp.maximum(-x, 0.0) + jnp.log1p(jnp.exp(-jnp.abs(x))))


def _sigmoid(x):
    return 1.0 / (1.0 + jnp.exp(-x))


def _silu(x):
    return x * _sigmoid(x)


def _gelu_tanh(x):
    return 0.5 * x * (1.0 + jnp.tanh(math.sqrt(2.0 / math.pi) * (x + 0.044715 * (x * x * x))))


def _rms(x, g):
    return x * lax.rsqrt(jnp.mean(x * x, axis=-1, keepdims=True) + EPS) * g


def _rope_lanes(x, cos, sina, sinb):
    half = QK_ROPE // 2
    return x * cos + pltpu.roll(x, half, 1) * sina + pltpu.roll(x, x.shape[1] - half, 1) * sinb


def _params(sem):
    return pltpu.CompilerParams(dimension_semantics=sem, vmem_limit_bytes=VMEM_LIMIT)


def _lb_kernel(gamma_ref, lb_ref, loglb_ref, log1m_ref):
    g = gamma_ref[...]
    e = jnp.exp(g - jnp.max(g, axis=0, keepdims=True))
    sm = e / jnp.sum(e, axis=0, keepdims=True)
    depth = g.shape[0]
    row = lax.broadcasted_iota(jnp.int32, g.shape, 0)
    cum = jnp.zeros_like(g)
    for u in range(depth):
        cum = cum + jnp.where(row >= u, sm[u:u + 1, :], 0.0)
    lb = cum - cum[0:1, :]
    lb_ref[...] = lb
    loglb_ref[...] = jnp.log(lb)
    log1m_ref[...] = jnp.log1p(-lb)


def _lower_bounds(hgrn_gamma):
    shp = jax.ShapeDtypeStruct(hgrn_gamma.shape, F32)
    return pl.pallas_call(_lb_kernel, out_shape=[shp, shp, shp], name="hgrn_lower_bounds")(hgrn_gamma)


def _ada_kernel(c_ref, w_ref, b_ref, o_ref):
    c = c_ref[...]
    o_ref[...] = _dot(_silu(c).astype(BF16), w_ref[...].astype(BF16)) + b_ref[...]


def _ada_mod(c_all, ada_w, ada_b):
    depth, d, n = ada_w.shape
    rows = c_all.shape[0]
    tn = 1536
    return pl.pallas_call(
        _ada_kernel,
        grid=(depth, n // tn),
        in_specs=[pl.BlockSpec((rows, d), lambda l, j: (0, 0)),
                  pl.BlockSpec((None, d, tn), lambda l, j: (l, 0, j)),
                  pl.BlockSpec((None, 1, tn), lambda l, j: (l, 0, j))],
        out_specs=pl.BlockSpec((None, rows, tn), lambda l, j: (l, 0, j)),
        out_shape=jax.ShapeDtypeStruct((depth, rows, n), F32),
        compiler_params=_params(("arbitrary", "arbitrary")),
        name="ada_modulation",
    )(c_all, ada_w, ada_b.reshape(depth, 1, n))


def _inproj_kernel(*refs, prompt, tiles_per_seq):
    if prompt:
        (x_ref, sh_ref, sc_ref, g_ref, w_ref, wcat_ref, bf_ref, kvg_ref, cos_ref, sina_ref, sinb_ref,
         gw_ref, gb_ref,
         oa_ref, bq_ref, bk_ref, bv_ref, bk16_ref, bv16_ref, logf_ref, cum_ref, call_ref, qcat_ref, kcat_ref,
         ckv_ref, krope_ref, carry_ref) = refs
    else:
        (x_ref, sh_ref, sc_ref, g_ref, w_ref, wcat_ref, bf_ref, kvg_ref, cos_ref, sina_ref, sinb_ref,
         gw_ref, gb_ref,
         oa_ref, av_ref, bq_ref, bk_ref, bv_ref, logf_ref, call_ref, qcat_ref, kcat_ref, ckv_ref,
         krope_ref, vbuf_ref) = refs
    i = pl.program_id(0)
    x = x_ref[...]
    tm = x.shape[0]
    h = _rms(x, g_ref[...]) * (1.0 + sc_ref[...]) + sh_ref[...]
    hb = h.astype(BF16)

    def seg(name, width):
        return _dot(hb, w_ref[:, SEG[name]:SEG[name] + width])

    cos, sina, sinb = cos_ref[...], sina_ref[...], sinb_ref[...]

    a_u = _gelu_tanh(seg("a_u", GROUP_W))
    a_v = _gelu_tanh(seg("a_v", GROUP_W))
    head = _lane_head((CHUNK_A if prompt else tm, GROUP_W))
    if prompt:
        r = lax.broadcasted_iota(jnp.int32, (CHUNK_A, CHUNK_A), 0)
        c = lax.broadcasted_iota(jnp.int32, (CHUNK_A, CHUNK_A), 1)
        for ch in range(tm // CHUNK_A):
            rows = slice(ch * CHUNK_A, (ch + 1) * CHUNK_A)
            vb = a_v[rows].astype(BF16)
            mixed = gb_ref[...]
            for hh in range(N_HEADS):
                wc = jnp.where(r >= c, gw_ref[hh], 0.0).astype(BF16)
                mixed = mixed + jnp.where(head == hh, _dot(wc, vb), 0.0)
            oa_ref[rows, :] = a_u[rows] * mixed
    else:
        @pl.when(i == 0)
        def _():
            vbuf_ref[...] = jnp.zeros_like(vbuf_ref)
        vbuf_ref[i] = a_v
        mixed = jnp.broadcast_to(gb_ref[...], a_v.shape)
        for s in range(vbuf_ref.shape[0]):
            mixed = mixed + gw_ref[s:s + 1, :] * vbuf_ref[s]
        oa_ref[...] = a_u * mixed
        av_ref[...] = a_v

    bq_ref[...] = (seg("b_q", GROUP_W) * (HEAD_DIM ** -0.5)).astype(BF16)
    b_k = seg("b_k", GROUP_W)
    b_v = seg("b_v", GROUP_W)
    bk_ref[...] = b_k
    bv_ref[...] = b_v
    if prompt:
        bk16_ref[...] = b_k.astype(BF16)
        bv16_ref[...] = b_v.astype(BF16)
    lane = lax.broadcasted_iota(jnp.int32, (tm, LANES), 1)
    logf = jnp.where(lane < N_HEADS, _log_sigmoid(seg("b_f", LANES) + bf_ref[...]), 0.0)
    logf_ref[...] = logf
    if prompt:
        @pl.when(i % tiles_per_seq == 0)
        def _():
            carry_ref[...] = jnp.zeros_like(carry_ref)
        rr = lax.broadcasted_iota(jnp.int32, (tm, tm), 0)
        cc = lax.broadcasted_iota(jnp.int32, (tm, tm), 1)
        tri = jnp.where(rr >= cc, 1.0, 0.0).astype(BF16)
        cum = _dot3_left(tri, logf) + carry_ref[...]
        cum_ref[...] = cum
        carry_ref[...] = cum[tm - 1:tm, :]

    call_ref[...] = seg("c_all", 4 * GROUP_W)

    d_qn = seg("d_qn", GROUP_W)
    d_qr = _rope_lanes(seg("d_qr", LANES), cos, sina, sinb)
    qin = jnp.concatenate([d_qn, d_qr], axis=-1).astype(BF16)
    qcat_ref[...] = (_dot(qin, wcat_ref[...]) * ((QK_NOPE + QK_ROPE) ** -0.5)).astype(BF16)
    ckv = _rms(seg("d_ckv", KV_RANK), kvg_ref[...])
    krope = _rope_lanes(seg("d_kr", LANES), cos, sina, sinb)
    ckv_ref[...] = ckv
    krope_ref[...] = krope
    kcat_ref[...] = jnp.concatenate([ckv, krope], axis=-1).astype(BF16)


def _inproj(x, sh, sc, g, w, wcat, bf_pad, kvg, cos, sina, sinb, gw, gb, *, prompt, seq_len, tm):
    t, d = x.shape
    tiles = t // tm
    row = lambda width: pl.BlockSpec((tm, width), lambda i: (i, 0))
    full = lambda a: pl.BlockSpec(a.shape, lambda i: (0,) * a.ndim)
    if prompt:
        tps = seq_len // tm
        mod_spec = pl.BlockSpec((None, 1, d), lambda i: (i // tps, 0, 0))
        tab_spec = pl.BlockSpec((tm, LANES), lambda i: (i % tps, 0))
        gw_spec, gb_spec = full(gw), full(gb)
    else:
        tps = 1
        mod_spec = pl.BlockSpec((tm, d), lambda i: (0, 0))
        tab_spec = pl.BlockSpec((None, 1, LANES), lambda i: (i, 0, 0))
        gw_spec = pl.BlockSpec((None,) + gw.shape[1:], lambda i: (i, 0, 0))
        gb_spec = pl.BlockSpec((None,) + gb.shape[1:], lambda i: (i, 0, 0))
    in_specs = [row(d), mod_spec, mod_spec, full(g), full(w), full(wcat), full(bf_pad), full(kvg),
                tab_spec, tab_spec, tab_spec, gw_spec, gb_spec]
    f = lambda width, dt=F32: jax.ShapeDtypeStruct((t, width), dt)
    outs = [("oa", f(GROUP_W))]
    if not prompt:
        outs.append(("av", f(GROUP_W)))
    outs += [("bq", f(GROUP_W, BF16)), ("bk", f(GROUP_W)), ("bv", f(GROUP_W))]
    if prompt:
        outs += [("bk16", f(GROUP_W, BF16)), ("bv16", f(GROUP_W, BF16))]
    outs.append(("logf", f(LANES)))
    if prompt:
        outs.append(("cum", f(LANES)))
    outs += [("call", f(4 * GROUP_W)), ("qcat", f(N_HEADS * 2 * KV_RANK, BF16)), ("kcat", f(2 * KV_RANK, BF16)),
             ("ckv", f(KV_RANK)), ("krope", f(LANES))]
    scratch = [pltpu.VMEM((1, LANES), F32)] if prompt else [pltpu.VMEM((tiles, tm, GROUP_W), F32)]
    res = pl.pallas_call(
        functools.partial(_inproj_kernel, prompt=prompt, tiles_per_seq=tps),
        grid=(tiles,),
        in_specs=in_specs,
        out_specs=[row(s.shape[1]) for _, s in outs],
        out_shape=[s for _, s in outs],
        scratch_shapes=scratch,
        compiler_params=_params(("arbitrary",)),
        name="inproj_prompt" if prompt else "inproj_sample",
    )(x, sh, sc, g, w, wcat, bf_pad, kvg, cos, sina, sinb, gw, gb)
    return dict(zip([n for n, _ in outs], res))


def _online_softmax_step(s, m_ref, l_ref, acc_ref, pv):
    m_prev = m_ref[...]
    m_new = jnp.maximum(m_prev, jnp.max(s, axis=-1, keepdims=True))
    alpha = jnp.exp(m_prev - m_new)
    p = jnp.exp(s - m_new)
    l_ref[...] = alpha * l_ref[...] + jnp.sum(p, axis=-1, keepdims=True)
    acc_ref[...] = alpha * acc_ref[...] + pv(p.astype(BF16))
    m_ref[...] = m_new


def _init_softmax(m_ref, l_ref, acc_ref):
    m_ref[...] = jnp.full_like(m_ref, -jnp.inf)
    l_ref[...] = jnp.zeros_like(l_ref)
    acc_ref[...] = jnp.zeros_like(acc_ref)


def _head_softmax_update(s, m_ref, l_ref, acc_ref, vb):
    m_prev = m_ref[...]
    m_new = jnp.maximum(m_prev, jnp.max(s, axis=-1, keepdims=True))
    alpha = jnp.exp(m_prev - m_new)
    p = jnp.exp(s - _lane_repeat(m_new, s.shape[1] // LANES))
    l_ref[...] = alpha * l_ref[...] + jnp.sum(p, axis=-1, keepdims=True)
    acc_ref[...] = _lane_repeat(alpha, acc_ref.shape[1] // LANES) * acc_ref[...] + _dot(p.astype(BF16), vb)
    m_ref[...] = m_new


def _lane_repeat(x, n):
    return x if n == 1 else pltpu.repeat(x, n, 1)


def _causal_kv_sweep(i, tq, tk, block):
    n_full = (i * tq) // tk

    def body(j, carry):
        block(j, None)
        return carry

    lax.fori_loop(0, n_full, body, 0)
    qpos = i * tq + lax.broadcasted_iota(jnp.int32, (tq, tk), 0)
    kpos = n_full * tk + lax.broadcasted_iota(jnp.int32, (tq, tk), 1)
    block(n_full, kpos <= qpos)


def _fox_prefill_kernel(q_ref, k_ref, v_ref, cq_ref, ck_ref, o_ref, qst_ref, *scratch, tq, tk):
    i = pl.program_id(1)
    ms, ls, accs = scratch[0:N_HEADS], scratch[N_HEADS:2 * N_HEADS], scratch[2 * N_HEADS:3 * N_HEADS]
    q = q_ref[...]
    head = _lane_head(q.shape)
    for hh in range(N_HEADS):
        _init_softmax(ms[hh], ls[hh], accs[hh])
        qst_ref[hh] = jnp.where(head == hh, q, jnp.zeros_like(q))
    cq = cq_ref[...]

    def block(j, ok):
        start = pl.multiple_of(j * tk, tk)
        kb = k_ref[pl.ds(start, tk), :]
        vb = v_ref[pl.ds(start, tk), :]
        ck = ck_ref[:, pl.ds(start, tk)]
        for hh in range(N_HEADS):
            s = _dot_nt(qst_ref[hh], kb) + (cq[:, hh:hh + 1] - ck[hh:hh + 1, :])
            if ok is not None:
                s = jnp.where(ok, s, -jnp.inf)
            _head_softmax_update(s, ms[hh], ls[hh], accs[hh], vb)

    _causal_kv_sweep(i, tq, tk, block)
    out = jnp.zeros((tq, GROUP_W), F32)
    for hh in range(N_HEADS):
        out = out + jnp.where(head == hh, accs[hh][...] / _lane_repeat(ls[hh][...], GROUP_W // LANES), 0.0)
    o_ref[...] = out


def _fox_prefill(bq, bk16, bv16, cum, cum_t, *, batch, seq_len, tq, tk):
    assert tk % tq == 0
    nq = seq_len // tq
    seq_rows = pl.BlockSpec((seq_len, GROUP_W), lambda b, i: (b, 0))
    stat = [pltpu.VMEM((tq, LANES), F32)] * (2 * N_HEADS)
    return pl.pallas_call(
        functools.partial(_fox_prefill_kernel, tq=tq, tk=tk),
        grid=(batch, nq),
        in_specs=[pl.BlockSpec((tq, GROUP_W), lambda b, i: (b * nq + i, 0)),
                  seq_rows, seq_rows,
                  pl.BlockSpec((tq, LANES), lambda b, i: (b * nq + i, 0)),
                  pl.BlockSpec((None, N_HEADS, seq_len), lambda b, i: (b, 0, 0))],
        out_specs=pl.BlockSpec((tq, GROUP_W), lambda b, i: (b * nq + i, 0)),
        out_shape=jax.ShapeDtypeStruct((batch * seq_len, GROUP_W), F32),
        scratch_shapes=[pltpu.VMEM((N_HEADS, tq, GROUP_W), BF16)] + stat + [pltpu.VMEM((tq, GROUP_W), F32)] * N_HEADS,
        compiler_params=_params(("arbitrary", "arbitrary")),
        name="fox_prefill",
    )(bq, bk16, bv16, cum, cum_t)


def _mla_prefill_kernel(q_ref, kv_ref, wuv_ref, o_ref, *scratch, tq, tk):
    i = pl.program_id(1)
    dq = 2 * KV_RANK
    ms, ls, accs = scratch[0:N_HEADS], scratch[N_HEADS:2 * N_HEADS], scratch[2 * N_HEADS:3 * N_HEADS]
    for hh in range(N_HEADS):
        _init_softmax(ms[hh], ls[hh], accs[hh])

    def block(j, ok):
        kv = kv_ref[pl.ds(pl.multiple_of(j * tk, tk), tk), :]
        vb = kv[:, :KV_RANK]
        for hh in range(N_HEADS):
            s = _dot_nt(q_ref[:, hh * dq:(hh + 1) * dq], kv)
            if ok is not None:
                s = jnp.where(ok, s, -jnp.inf)
            _head_softmax_update(s, ms[hh], ls[hh], accs[hh], vb)

    _causal_kv_sweep(i, tq, tk, block)
    out = jnp.zeros((tq, GROUP_W), F32)
    for hh in range(N_HEADS):
        o_lat = (accs[hh][...] / ls[hh][...]).astype(BF16)
        out = out + _dot(o_lat, wuv_ref[hh])
    o_ref[...] = out


def _mla_prefill(qcat, kcat, wuv_pad, *, batch, seq_len, tq, tk):
    assert tk % tq == 0
    nq = seq_len // tq
    stat = [pltpu.VMEM((tq, LANES), F32)] * (2 * N_HEADS)
    return pl.pallas_call(
        functools.partial(_mla_prefill_kernel, tq=tq, tk=tk),
        grid=(batch, nq),
        in_specs=[pl.BlockSpec((tq, qcat.shape[1]), lambda b, i: (b * nq + i, 0)),
                  pl.BlockSpec((seq_len, kcat.shape[1]), lambda b, i: (b, 0)),
                  pl.BlockSpec(wuv_pad.shape, lambda b, i: (0, 0, 0))],
        out_specs=pl.BlockSpec((tq, GROUP_W), lambda b, i: (b * nq + i, 0)),
        out_shape=jax.ShapeDtypeStruct((batch * seq_len, GROUP_W), F32),
        scratch_shapes=stat + [pltpu.VMEM((tq, KV_RANK), F32)] * N_HEADS,
        compiler_params=_params(("arbitrary", "arbitrary")),
        name="mla_prefill",
    )(qcat, kcat, wuv_pad)


def _hgrn_gates(zf, lb, loglb, log1m):
    a = loglb
    b = log1m + _log_sigmoid(zf)
    log_f = jnp.maximum(a, b) + jnp.log1p(jnp.exp(-jnp.abs(a - b)))
    k_in = (1.0 - lb) * _sigmoid(-zf)
    return log_f, k_in


def _head_rms_rows(o, gain):
    head = _lane_head(o.shape)
    sq = o * o
    ms = jnp.zeros_like(o)
    for hh in range(N_HEADS):
        ms = ms + jnp.where(head == hh, jnp.sum(jnp.where(head == hh, sq, 0.0), axis=-1, keepdims=True), 0.0)
    return o * lax.rsqrt(ms * (1.0 / HEAD_DIM) + EPS) * gain


def _hgrn_prefill_kernel(call_ref, lb_ref, loglb_ref, log1m_ref, gain_ref, o_ref, st_out_ref, st_ref, *, rows, chunk):
    i = pl.program_id(1)

    @pl.when(i == 0)
    def _():
        st_ref[...] = jnp.zeros_like(st_ref)

    w = GROUP_W
    rr = lax.broadcasted_iota(jnp.int32, (chunk, chunk), 0)
    cc = lax.broadcasted_iota(jnp.int32, (chunk, chunk), 1)
    causal = rr >= cc
    tri = jnp.where(causal, 1.0, 0.0).astype(BF16)
    head = _lane_head((chunk, w))
    bd = (lax.broadcasted_iota(jnp.int32, (w, w), 0) // HEAD_DIM) == (lax.broadcasted_iota(jnp.int32, (w, w), 1) // HEAD_DIM)
    half = chunk // 2
    for c in range(rows // chunk):
        rs = slice(c * chunk, (c + 1) * chunk)
        q = call_ref[rs, 0:w]
        zf = call_ref[rs, w:2 * w]
        v = call_ref[rs, 2 * w:3 * w]
        gate = call_ref[rs, 3 * w:4 * w]
        log_f, k_in = _hgrn_gates(zf, lb_ref[...], loglb_ref[...], log1m_ref[...])
        b = _dot3_left(tri, log_f)
        ref = b[half - 1:half, :]
        b_last = b[chunk - 1:chunk, :]
        qt = (q * jnp.exp(b - ref))
        kt = (k_in * jnp.exp(ref - b)).astype(BF16)
        vb = v.astype(BF16)
        st = st_ref[...]
        o = _dot_nt((q * jnp.exp(b)).astype(BF16), st.astype(BF16))
        for hh in range(N_HEADS):
            a = _dot_nt(jnp.where(head == hh, qt, 0.0).astype(BF16), kt)
            a = jnp.where(causal, a, 0.0).astype(BF16)
            o = o + jnp.where(head == hh, _dot(a, vb), 0.0)
        kd = (k_in * jnp.exp(b_last - b)).astype(BF16)
        st_ref[...] = st * jnp.exp(b_last) + jnp.where(bd, _dot_tn(vb, kd), 0.0)
        o_ref[rs, :] = _head_rms_rows(o, gain_ref[...]) * _silu(gate)

    st_out_ref[...] = st_ref[...]


def _hgrn_prefill(call, lb, loglb, log1m, gain, *, batch, seq_len, rows):
    tps = seq_len // rows
    vec = pl.BlockSpec((1, GROUP_W), lambda b, i: (0, 0))
    return pl.pallas_call(
        functools.partial(_hgrn_prefill_kernel, rows=rows, chunk=HGRN_CHUNK),
        grid=(batch, tps),
        in_specs=[pl.BlockSpec((rows, 4 * GROUP_W), lambda b, i: (b * tps + i, 0)), vec, vec, vec, vec],
        out_specs=[pl.BlockSpec((rows, GROUP_W), lambda b, i: (b * tps + i, 0)),
                   pl.BlockSpec((None, GROUP_W, GROUP_W), lambda b, i: (b, 0, 0))],
        out_shape=[jax.ShapeDtypeStruct((batch * seq_len, GROUP_W), F32),
                   jax.ShapeDtypeStruct((batch, GROUP_W, GROUP_W), F32)],
        scratch_shapes=[pltpu.VMEM((GROUP_W, GROUP_W), F32)],
        compiler_params=_params(("arbitrary", "arbitrary")),
        name="hgrn_prefill",
    )(call, lb, loglb, log1m, gain)


def _hgrn_decode_kernel(q_ref, zf_ref, v_ref, gate_ref, lb_ref, loglb_ref, log1m_ref, gain_ref, s_ref,
                        o_ref, s_out_ref, f_scr, k_scr, st_scr):
    n_tok = q_ref.shape[0]
    dk = s_ref.shape[0]
    log_f, k_in = _hgrn_gates(zf_ref[...], lb_ref[...], loglb_ref[...], log1m_ref[...])
    f_scr[...] = jnp.exp(log_f)
    k_scr[...] = k_in
    st_scr[...] = s_ref[...]
    for t in range(n_tok):
        v_t = v_ref[t]

        def body(k, o_acc):
            f_row = f_scr[t, pl.ds(k, 1), :]
            k_row = k_scr[t, pl.ds(k, 1), :]
            q_row = q_ref[t, pl.ds(k, 1), :]
            s_new = st_scr[k] * f_row + k_row * v_t
            st_scr[k] = s_new
            return o_acc + s_new * q_row

        o = lax.fori_loop(0, dk, body, jnp.zeros(v_t.shape, F32))
        o = o * lax.rsqrt(jnp.mean(o * o, axis=0, keepdims=True) + EPS) * gain_ref[...]
        o_ref[t] = o * _silu(gate_ref[t])
    s_out_ref[...] = st_scr[...]


def _hgrn_decode(call_t, lb_c, loglb_c, log1m_c, gain_c, state):
    n_tok, _, nb = call_t.shape
    nh, dk, dv, _ = state.shape
    part = lambda p: pl.BlockSpec((n_tok, HEAD_DIM, nb), lambda h: (0, p * N_HEADS + h, 0))
    col = pl.BlockSpec((HEAD_DIM, 1), lambda h: (h, 0))
    return pl.pallas_call(
        _hgrn_decode_kernel,
        grid=(nh,),
        in_specs=[part(0), part(1), part(2), part(3), col, col, col,
                  pl.BlockSpec((HEAD_DIM, 1), lambda h: (0, 0)),
                  pl.BlockSpec((None, dk, dv, nb), lambda h: (h, 0, 0, 0))],
        out_specs=[pl.BlockSpec((n_tok, HEAD_DIM, nb), lambda h: (0, h, 0)),
                   pl.BlockSpec((None, dk, dv, nb), lambda h: (h, 0, 0, 0))],
        out_shape=[jax.ShapeDtypeStruct((n_tok, GROUP_W, nb), F32),
                   jax.ShapeDtypeStruct(state.shape, F32)],
        scratch_shapes=[pltpu.VMEM((n_tok, HEAD_DIM, nb), F32),
                        pltpu.VMEM((n_tok, HEAD_DIM, nb), F32),
                        pltpu.VMEM((dk, dv, nb), F32)],
        compiler_params=_params(("arbitrary",)),
        name="hgrn_decode",
    )(call_t, call_t, call_t, call_t, lb_c, loglb_c, log1m_c, gain_c, state)


Q_PAD = 8


def _fox_decode_kernel(pt_ref, q_ref, kn_ref, vn_ref, lfn_ref, lfnt_ref, *rest, pps):
    k_refs = rest[0:pps]
    v_refs = rest[pps:2 * pps]
    lf_refs = rest[2 * pps:3 * pps]
    o_ref = rest[3 * pps]
    qbd_ref, kn_pad, vn_pad, m_ref, l_ref, acc_ref, carry_ref, erow_ref = rest[3 * pps + 1:]
    b, c = pl.program_id(0), pl.program_id(1)
    page = kn_pad.shape[0]
    nrow = N_HEADS * Q_PAD
    su = lax.broadcasted_iota(jnp.int32, (page, page), 0)
    sv = lax.broadcasted_iota(jnp.int32, (page, page), 1)
    later = jnp.where(su > sv, 1.0, 0.0).astype(BF16)

    @pl.when((b == 0) & (c == 0))
    def _():
        kn_pad[...] = jnp.zeros_like(kn_pad)
        vn_pad[...] = jnp.zeros_like(vn_pad)

    @pl.when(c == 0)
    def _():
        _init_softmax(m_ref, l_ref, acc_ref)
        q = q_ref[...].astype(F32)
        head = _lane_head(q.shape)
        for hh in range(N_HEADS):
            qbd_ref[hh * Q_PAD:(hh + 1) * Q_PAD, :] = jnp.where(head == hh, q, 0.0)
        kn_pad[0:Q_PAD, :] = kn_ref[...]
        vn_pad[0:Q_PAD, :] = vn_ref[...]
        lfn = lfn_ref[...]
        trow = lax.broadcasted_iota(jnp.int32, lfn.shape, 0)
        e_q = jnp.zeros_like(lfn)
        for u in range(1, Q_PAD):
            e_q = e_q + jnp.where(trow < u, lfn[u:u + 1, :], 0.0)
        for hh in range(N_HEADS):
            erow_ref[hh * Q_PAD:(hh + 1) * Q_PAD, :] = e_q[:, hh:hh + 1]
        lfnt = lfnt_ref[...]
        e_k = _dot3_right(lfnt, later)
        carry_ref[...] = jnp.sum(lfnt, axis=-1, keepdims=True)
        bias = jnp.concatenate([jnp.broadcast_to(e_k[hh:hh + 1, :], (Q_PAD, page)) for hh in range(N_HEADS)], axis=0)
        s = _dot_nt(qbd_ref[...].astype(BF16), kn_pad[...].astype(BF16)) + (bias - erow_ref[...])
        col = lax.broadcasted_iota(jnp.int32, (nrow, page), 1)
        qtok = lax.broadcasted_iota(jnp.int32, (nrow, page), 0) % Q_PAD
        s = jnp.where(col <= qtok, s, -jnp.inf)
        _online_softmax_step(s, m_ref, l_ref, acc_ref, lambda p: _dot(p, vn_pad[...].astype(BF16)))

    qbd = qbd_ref[...].astype(BF16)
    erow = erow_ref[...]
    carry = carry_ref[...]
    parts = []
    for jj in range(pps - 1, -1, -1):
        lf = lf_refs[jj][...]
        e_k = _dot3_right(lf, later) + carry
        carry = carry + jnp.sum(lf, axis=-1, keepdims=True)
        bias = jnp.concatenate([jnp.broadcast_to(e_k[hh:hh + 1, :], (Q_PAD, page)) for hh in range(N_HEADS)], axis=0)
        kt = k_refs[jj][...].reshape(GROUP_W, page).astype(BF16)
        parts.append(_dot(qbd, kt) + (bias - erow))
    carry_ref[...] = carry
    s = jnp.concatenate(parts, axis=-1)

    def pv(p):
        out = jnp.zeros((nrow, GROUP_W), F32)
        for n, jj in enumerate(range(pps - 1, -1, -1)):
            vt = v_refs[jj][...].reshape(GROUP_W, page).astype(BF16)
            out = out + _dot_nt(p[:, n * page:(n + 1) * page], vt)
        return out

    _online_softmax_step(s, m_ref, l_ref, acc_ref, pv)

    @pl.when(c == pl.num_programs(1) - 1)
    def _():
        head = _lane_head((Q_PAD, GROUP_W))
        out = jnp.zeros((Q_PAD, GROUP_W), F32)
        for hh in range(N_HEADS):
            rows = slice(hh * Q_PAD, (hh + 1) * Q_PAD)
            out = out + jnp.where(head == hh, acc_ref[rows, :] / l_ref[rows, :], 0.0)
        o_ref[...] = out


def _fox_decode(page_table, q, kn, vn, lfn, lfnt, cache_k, cache_v, cache_lf, *, layer, pps):
    nb, n_pages = page_table.shape
    page = cache_k.shape[-1]
    n_chunks = n_pages // pps
    seq = lambda a: pl.BlockSpec((None,) + a.shape[1:], lambda b, c, pt: (b,) + (0,) * (a.ndim - 1))

    def paged(a, jj):
        def idx(b, c, pt):
            return (layer, pt[b, (n_chunks - 1 - c) * pps + jj]) + (0,) * (a.ndim - 2)
        return pl.BlockSpec((None, None) + a.shape[2:], idx)

    in_specs = [seq(q), seq(kn), seq(vn), seq(lfn), seq(lfnt)]
    in_specs += [paged(cache_k, jj) for jj in range(pps)]
    in_specs += [paged(cache_v, jj) for jj in range(pps)]
    in_specs += [paged(cache_lf, jj) for jj in range(pps)]
    nrow = N_HEADS * Q_PAD
    return pl.pallas_call(
        functools.partial(_fox_decode_kernel, pps=pps),
        grid_spec=pltpu.PrefetchScalarGridSpec(
            num_scalar_prefetch=1,
            grid=(nb, n_chunks),
            in_specs=in_specs,
            out_specs=pl.BlockSpec((None, Q_PAD, GROUP_W), lambda b, c, pt: (b, 0, 0)),
            scratch_shapes=[pltpu.VMEM((nrow, GROUP_W), F32),
                            pltpu.VMEM((page, GROUP_W), F32),
                            pltpu.VMEM((page, GROUP_W), F32),
                            pltpu.VMEM((nrow, 1), F32),
                            pltpu.VMEM((nrow, 1), F32),
                            pltpu.VMEM((nrow, GROUP_W), F32),
                            pltpu.VMEM((N_HEADS, 1), F32),
                            pltpu.VMEM((nrow, 1), F32)]),
        out_shape=jax.ShapeDtypeStruct((nb, Q_PAD, GROUP_W), F32),
        compiler_params=_params(("arbitrary", "arbitrary")),
        name="fox_decode",
    )(page_table, q, kn, vn, lfn, lfnt, *([cache_k] * pps), *([cache_v] * pps), *([cache_lf] * pps))


def _mla_decode_kernel(pt_ref, q_ref, kn_ref, wuv_ref, *rest, pps):
    ckv_refs = rest[0:pps]
    kr_refs = rest[pps:2 * pps]
    o_ref = rest[2 * pps]
    qst_ref, kn_pad, m_ref, l_ref, acc_ref = rest[2 * pps + 1:]
    b, c = pl.program_id(0), pl.program_id(1)
    page = kn_pad.shape[0]
    nrow = N_HEADS * Q_PAD
    dq = 2 * KV_RANK

    @pl.when((b == 0) & (c == 0))
    def _():
        kn_pad[...] = jnp.zeros_like(kn_pad)

    @pl.when(c == 0)
    def _():
        _init_softmax(m_ref, l_ref, acc_ref)
        for hh in range(N_HEADS):
            qst_ref[hh * Q_PAD:(hh + 1) * Q_PAD, :] = q_ref[:, hh * dq:(hh + 1) * dq].astype(F32)
        kn_pad[0:Q_PAD, :] = kn_ref[...].astype(F32)
        s = _dot_nt(qst_ref[...].astype(BF16), kn_pad[...].astype(BF16))
        col = lax.broadcasted_iota(jnp.int32, (nrow, page), 1)
        qtok = lax.broadcasted_iota(jnp.int32, (nrow, page), 0) % Q_PAD
        s = jnp.where(col <= qtok, s, -jnp.inf)
        _online_softmax_step(s, m_ref, l_ref, acc_ref, lambda p: _dot(p, kn_pad[:, :KV_RANK].astype(BF16)))

    qst = qst_ref[...].astype(BF16)
    q_lat = qst[:, :KV_RANK]
    q_rope = qst[:, KV_RANK:]
    n_rope = kr_refs[0].shape[0]
    zpad = jnp.zeros((KV_RANK - n_rope, page), BF16)
    parts = []
    for jj in range(pps):
        ckv = ckv_refs[jj][...].astype(BF16)
        krt = jnp.concatenate([kr_refs[jj][...].astype(BF16), zpad], axis=0)
        parts.append(_dot_nt(q_lat, ckv) + _dot(q_rope, krt))
    s = jnp.concatenate(parts, axis=-1)

    def pv(p):
        out = jnp.zeros((nrow, KV_RANK), F32)
        for jj in range(pps):
            out = out + _dot(p[:, jj * page:(jj + 1) * page], ckv_refs[jj][...].astype(BF16))
        return out

    _online_softmax_step(s, m_ref, l_ref, acc_ref, pv)

    @pl.when(c == pl.num_programs(1) - 1)
    def _():
        out = jnp.zeros((Q_PAD, GROUP_W), F32)
        for hh in range(N_HEADS):
            rows = slice(hh * Q_PAD, (hh + 1) * Q_PAD)
            o_lat = (acc_ref[rows, :] / l_ref[rows, :]).astype(BF16)
            out = out + _dot(o_lat, wuv_ref[hh])
        o_ref[...] = out


def _mla_decode(page_table, q, kn, wuv_pad, cache_ckv, cache_krt, *, layer, pps):
    nb, n_pages = page_table.shape
    page = cache_ckv.shape[2]
    n_chunks = n_pages // pps
    seq = lambda a: pl.BlockSpec((None,) + a.shape[1:], lambda b, c, pt: (b,) + (0,) * (a.ndim - 1))

    def paged(a, jj):
        def idx(b, c, pt):
            return (layer, pt[b, c * pps + jj], 0, 0)
        return pl.BlockSpec((None, None) + a.shape[2:], idx)

    in_specs = [seq(q), seq(kn), pl.BlockSpec(wuv_pad.shape, lambda b, c, pt: (0, 0, 0))]
    in_specs += [paged(cache_ckv, jj) for jj in range(pps)]
    in_specs += [paged(cache_krt, jj) for jj in range(pps)]
    nrow = N_HEADS * Q_PAD
    return pl.pallas_call(
        functools.partial(_mla_decode_kernel, pps=pps),
        grid_spec=pltpu.PrefetchScalarGridSpec(
            num_scalar_prefetch=1,
            grid=(nb, n_chunks),
            in_specs=in_specs,
            out_specs=pl.BlockSpec((None, Q_PAD, GROUP_W), lambda b, c, pt: (b, 0, 0)),
            scratch_shapes=[pltpu.VMEM((nrow, 2 * KV_RANK), F32),
                            pltpu.VMEM((page, 2 * KV_RANK), F32),
                            pltpu.VMEM((nrow, 1), F32),
                            pltpu.VMEM((nrow, 1), F32),
                            pltpu.VMEM((nrow, KV_RANK), F32)]),
        out_shape=jax.ShapeDtypeStruct((nb, Q_PAD, GROUP_W), F32),
        compiler_params=_params(("arbitrary", "arbitrary")),
        name="mla_decode",
    )(page_table, q, kn, wuv_pad, *([cache_ckv] * pps), *([cache_krt] * pps))


FF_CHUNK = 1408
CARRY_ROWS = 8


def _mixffn_kernel(*refs, prompt, tiles_per_seq, final):
    if prompt:
        (x_ref, oa_ref, ob_ref, oc_ref, od_ref, g1_ref, sh2_ref, sc2_ref, g2_ref, ng_ref, wo_ref, wup_ref,
         cw_ref, cb_ref, wdn_ref, fg_ref, y_ref, nbuf_ref, ext_ref) = refs
    else:
        (x_ref, oa_ref, ob_ref, oc_ref, od_ref, g1_ref, sh2_ref, sc2_ref, g2_ref, ng_ref, wo_ref, wup_ref,
         cw_ref, cb_ref, wdn_ref, fg_ref, buf_ref, y_ref, gate_ref, hist_ref) = refs
    i = pl.program_id(0)
    x = x_ref[...]
    tm, d = x.shape
    d_ff = wdn_ref.shape[0]
    mix = jnp.zeros((tm, d), F32)
    for n, r in enumerate((oa_ref, ob_ref, oc_ref, od_ref)):
        mix = mix + _dot(r[...].astype(BF16), wo_ref[n * GROUP_W:(n + 1) * GROUP_W, :])
    x1 = x + g1_ref[...] * mix
    h2 = (_rms(x1, ng_ref[...]) * (1.0 + sc2_ref[...]) + sh2_ref[...]).astype(BF16)

    if prompt:
        @pl.when(i % tiles_per_seq == 0)
        def _():
            ext_ref[0:CARRY_ROWS, :] = jnp.zeros((CARRY_ROWS, d_ff), F32)
    else:
        @pl.when(i == 0)
        def _():
            hist_ref[...] = buf_ref[...]

    y = jnp.zeros((tm, d), F32)
    for c0 in range(0, d_ff, FF_CHUNK):
        cols = slice(c0, c0 + FF_CHUNK)
        gate = _dot(h2, wup_ref[:, cols])
        up = _dot(h2, wup_ref[:, d_ff + c0:d_ff + c0 + FF_CHUNK])
        if prompt:
            ext_ref[CARRY_ROWS:CARRY_ROWS + tm, cols] = gate
            prev1 = ext_ref[pl.ds(CARRY_ROWS - 1, tm), cols]
            prev2 = ext_ref[pl.ds(CARRY_ROWS - 2, tm), cols]
        else:
            prev2 = hist_ref[0, :, cols]
            prev1 = hist_ref[1, :, cols]
            hist_ref[0, :, cols] = prev1
            hist_ref[1, :, cols] = gate
            gate_ref[:, cols] = gate
        conv = cb_ref[:, cols] + prev2 * cw_ref[0:1, cols] + prev1 * cw_ref[1:2, cols] + gate * cw_ref[2:3, cols]
        act = (_silu(conv) * up).astype(BF16)
        y = y + _dot(act, wdn_ref[cols, :])
    if prompt:
        tail = ext_ref[tm:tm + CARRY_ROWS, :]
        ext_ref[0:CARRY_ROWS, :] = tail
        nbuf_ref[...] = tail[CARRY_ROWS - (CONV_W - 1):, :]
    x2 = x1 + g2_ref[...] * y
    if final:
        x2 = _rms(x2, fg_ref[...])
    y_ref[...] = x2


def _mixffn(x, oa, ob, oc, od, g1, sh2, sc2, g2, ng, wo, wup, cw, cb, wdn, fg, buf, *, prompt, seq_len, tm, final):
    t, d = x.shape
    d_ff = wdn.shape[0]
    tiles = t // tm
    row = lambda width: pl.BlockSpec((tm, width), lambda i: (i, 0))
    full = lambda a: pl.BlockSpec(a.shape, lambda i: (0,) * a.ndim)
    if prompt:
        tps = seq_len // tm
        mod_spec = pl.BlockSpec((None, 1, d), lambda i: (i // tps, 0, 0))
    else:
        tps = 1
        mod_spec = pl.BlockSpec((tm, d), lambda i: (0, 0))
    in_specs = [row(d), row(GROUP_W), row(GROUP_W), row(GROUP_W), row(GROUP_W), mod_spec, mod_spec, mod_spec,
                mod_spec, full(ng), full(wo), full(wup), full(cw), full(cb), full(wdn), full(fg)]
    args = [x, oa, ob, oc, od, g1, sh2, sc2, g2, ng, wo, wup, cw, cb, wdn, fg]
    if prompt:
        out_specs = [row(d), pl.BlockSpec((None, CONV_W - 1, d_ff), lambda i: (i // tps, 0, 0))]
        out_shape = [jax.ShapeDtypeStruct((t, d), F32), jax.ShapeDtypeStruct((t // seq_len, CONV_W - 1, d_ff), F32)]
        scratch = [pltpu.VMEM((CARRY_ROWS + tm, d_ff), F32)]
    else:
        in_specs.append(full(buf))
        args.append(buf)
        out_specs = [row(d), row(d_ff)]
        out_shape = [jax.ShapeDtypeStruct((t, d), F32), jax.ShapeDtypeStruct((t, d_ff), F32)]
        scratch = [pltpu.VMEM((CONV_W - 1, tm, d_ff), F32)]
    return pl.pallas_call(
        functools.partial(_mixffn_kernel, prompt=prompt, tiles_per_seq=tps, final=final),
        grid=(tiles,),
        in_specs=in_specs,
        out_specs=out_specs,
        out_shape=out_shape,
        scratch_shapes=scratch,
        compiler_params=_params(("arbitrary",)),
        name="mixffn_prompt" if prompt else "mixffn_sample",
    )(*args)


def _rope_tables(pos):
    half = QK_ROPE // 2
    inv_freq = ROPE_THETA ** (-jnp.arange(half, dtype=F32) / half)
    ang = pos.astype(F32)[:, None] * inv_freq[None, :]
    cos, sin, zero = jnp.cos(ang), jnp.sin(ang), jnp.zeros_like(ang)
    reps = LANES // QK_ROPE
    tile = lambda lo, hi: jnp.tile(jnp.concatenate([lo, hi], axis=-1), (1, reps))
    return tile(cos, cos), tile(zero, sin), tile(-sin, zero)


def _pack_w_in(w_in):
    sizes = (256, 256, 256, 256, 256, 4, 256, 256, 256, 256, 256, 128, 128, 32)
    offs = [0]
    for s in sizes:
        offs.append(offs[-1] + s)
    col = lambda n: w_in[:, :, offs[n]:offs[n + 1]]
    pad = lambda a: jnp.pad(a, ((0, 0), (0, 0), (0, LANES - a.shape[2])))
    parts = [col(0), col(1), col(2), col(3), col(4), col(6), col(7), col(8), col(9), col(10), col(11), col(12),
             pad(col(5)), pad(col(13))]
    return jnp.concatenate(parts, axis=-1).astype(BF16)


def _pack_w_qcat(w_uk):
    r = w_uk.shape[0]
    w = jnp.zeros((N_HEADS * QK_NOPE + N_HEADS * QK_ROPE, N_HEADS * 2 * r), F32)
    eye = jnp.eye(QK_ROPE, dtype=F32)
    for hh in range(N_HEADS):
        w = w.at[hh * QK_NOPE:(hh + 1) * QK_NOPE, hh * 2 * r:hh * 2 * r + r].set(w_uk[:, hh, :].T)
        w = w.at[N_HEADS * QK_NOPE + hh * QK_ROPE:N_HEADS * QK_NOPE + (hh + 1) * QK_ROPE,
                 hh * 2 * r + r:hh * 2 * r + r + QK_ROPE].set(eye)
    return w.astype(BF16)


def _pack_w_uv(w_uv):
    r = w_uv.shape[0]
    w = jnp.zeros((N_HEADS, r, GROUP_W), F32)
    for hh in range(N_HEADS):
        w = w.at[hh, :, hh * HEAD_DIM:(hh + 1) * HEAD_DIM].set(w_uv[:, hh, :])
    return w.astype(BF16)


def _pick(n, candidates):
    for c in candidates:
        if n % c == 0:
            return c
    return n


def kernel(x_prompt, x_sample, c_prompt, c_sample, cache_fox_k, cache_fox_v, cache_fox_logf, cache_mla_ckv, cache_mla_krope, state_hgrn, state_ffn_conv, page_table, ada_w, ada_b, norm_attn_g, norm_ffn_g, w_in, gmlp_ws, gmlp_b, fox_bf, hgrn_gamma, hgrn_norm_g, mla_kv_norm_g, mla_w_uk, mla_w_uv, w_out, ffn_w_up, ffn_conv_w, ffn_conv_b, ffn_w_down, final_norm_g):
    depth = w_in.shape[0]
    bp, seq_len, d = x_prompt.shape
    nb, n_tok, _ = x_sample.shape
    n_pages = page_table.shape[1]
    page = cache_fox_k.shape[2]
    d_ff = ffn_w_down.shape[1]
    past_len = n_pages * page
    tp = bp * seq_len

    tm_p = _pick(seq_len, (256, 128))
    tq = _pick(seq_len, (256, 128))
    tk = _pick(seq_len, (512, 256, 128))
    hg_rows = _pick(seq_len, (256, 128, 64))
    pps = _pick(n_pages, (32, 16, 8, 4, 2, 1))

    w_in_p = _pack_w_in(w_in)
    w_out_b = w_out.astype(BF16)
    w_up_b = ffn_w_up.astype(BF16)
    w_dn_b = ffn_w_down.astype(BF16)
    bf_pad = jnp.pad(fox_bf, ((0, 0), (0, LANES - fox_bf.shape[1])))
    gain_row = jnp.tile(hgrn_norm_g, (1, N_HEADS))
    lb, loglb, log1m = _lower_bounds(hgrn_gamma.astype(F32))
    cos_p, sina_p, sinb_p = _rope_tables(jnp.arange(seq_len))
    cos_s, sina_s, sinb_s = (a.reshape(n_tok, 1, LANES) for a in _rope_tables(past_len + jnp.arange(n_tok)))
    gb_p = jnp.repeat(jnp.swapaxes(gmlp_b, 1, 2), HEAD_DIM, axis=2)
    gw_s = jnp.repeat(jnp.transpose(gmlp_ws[:, :, :n_tok, :n_tok], (0, 2, 3, 1)), HEAD_DIM, axis=3)
    gw_s = jnp.where((jnp.arange(n_tok)[:, None] >= jnp.arange(n_tok)[None, :])[None, :, :, None], gw_s, 0.0)
    gb_s = gb_p[:, :n_tok].reshape(depth, n_tok, 1, GROUP_W)

    rows_c = bp + nb
    rows_pad = -(-rows_c // 8) * 8
    c_all = jnp.pad(jnp.concatenate([c_prompt, c_sample], axis=0), ((0, rows_pad - rows_c), (0, 0)))
    mod = _ada_mod(c_all, ada_w, ada_b)

    ck_view = jnp.transpose(cache_fox_k, (0, 1, 3, 4, 2))
    cv_view = jnp.transpose(cache_fox_v, (0, 1, 3, 4, 2))
    clf_view = jnp.transpose(cache_fox_logf, (0, 1, 3, 2))
    ckr_view = jnp.transpose(cache_mla_krope, (0, 1, 3, 2))
    st_view = jnp.transpose(state_hgrn, (0, 2, 3, 4, 1))

    hp = x_prompt.reshape(tp, d)
    hs = jnp.swapaxes(x_sample, 0, 1).reshape(n_tok * nb, d)
    new_p = {k: [] for k in ("fox_k", "fox_v", "fox_logf", "mla_ckv", "mla_krope", "hgrn", "ffn_conv")}
    new_s = {k: [] for k in ("fox_k", "fox_v", "fox_logf", "mla_ckv", "mla_krope", "hgrn", "ffn_conv", "gmlp_v")}

    def seq_major(a):
        return jnp.swapaxes(a.reshape(n_tok, nb, a.shape[-1]), 0, 1)

    def pad_q(a):
        return jnp.pad(a, ((0, 0), (0, Q_PAD - n_tok), (0, 0)))

    for l in range(depth):
        m6 = mod[l].reshape(rows_pad, 6, d)
        modp = [m6[:bp, n].reshape(bp, 1, d) for n in range(6)]
        mods = [m6[bp:bp + nb, n] for n in range(6)]
        wcat = _pack_w_qcat(mla_w_uk[l])
        wuv_pad = _pack_w_uv(mla_w_uv[l])
        vec = lambda a: a[l].reshape(1, -1)
        final = l == depth - 1

        zp = _inproj(hp, modp[0], modp[1], vec(norm_attn_g), w_in_p[l], wcat, vec(bf_pad), vec(mla_kv_norm_g),
                     cos_p, sina_p, sinb_p, gmlp_ws[l], gb_p[l], prompt=True, seq_len=seq_len, tm=tm_p)
        cum_t = jnp.swapaxes(zp["cum"][:, :N_HEADS].reshape(bp, seq_len, N_HEADS), 1, 2)
        ob = _fox_prefill(zp["bq"], zp["bk16"], zp["bv16"], zp["cum"], cum_t, batch=bp, seq_len=seq_len, tq=tq, tk=tk)
        oc, st_p = _hgrn_prefill(zp["call"], vec(lb), vec(loglb), vec(log1m), vec(gain_row),
                                 batch=bp, seq_len=seq_len, rows=hg_rows)
        od = _mla_prefill(zp["qcat"], zp["kcat"], wuv_pad, batch=bp, seq_len=seq_len, tq=tq, tk=tk)
        hp, nbuf_p = _mixffn(hp, zp["oa"], ob, oc, od, modp[2], modp[3], modp[4], modp[5], vec(norm_ffn_g),
                             w_out_b[l], w_up_b[l], ffn_conv_w[l], vec(ffn_conv_b), w_dn_b[l],
                             final_norm_g.reshape(1, d), None, prompt=True, seq_len=seq_len, tm=tm_p, final=final)
        new_p["fox_k"].append(zp["bk"])
        new_p["fox_v"].append(zp["bv"])
        new_p["fox_logf"].append(zp["logf"][:, :N_HEADS])
        new_p["mla_ckv"].append(zp["ckv"])
        new_p["mla_krope"].append(zp["krope"][:, :QK_ROPE])
        st4 = st_p.reshape(bp, N_HEADS, HEAD_DIM, N_HEADS, HEAD_DIM)
        new_p["hgrn"].append(jnp.stack([jnp.swapaxes(st4[:, hh, :, hh, :], 1, 2) for hh in range(N_HEADS)], axis=1))
        new_p["ffn_conv"].append(nbuf_p)

        zs = _inproj(hs, mods[0], mods[1], vec(norm_attn_g), w_in_p[l], wcat, vec(bf_pad), vec(mla_kv_norm_g),
                     cos_s, sina_s, sinb_s, gw_s[l], gb_s[l], prompt=False, seq_len=n_tok, tm=nb)
        q_s, kn_s, vn_s = (pad_q(seq_major(zs[n])) for n in ("bq", "bk", "bv"))
        lf_s = seq_major(zs["logf"])
        lfn = pad_q(lf_s)
        lfnt = jnp.pad(jnp.swapaxes(lf_s[:, :, :N_HEADS], 1, 2), ((0, 0), (0, 0), (0, page - n_tok)))
        ob_s = _fox_decode(page_table, q_s, kn_s, vn_s, lfn, lfnt, ck_view, cv_view, clf_view, layer=l, pps=pps)
        od_s = _mla_decode(page_table, pad_q(seq_major(zs["qcat"])), pad_q(seq_major(zs["kcat"])), wuv_pad,
                           cache_mla_ckv, ckr_view, layer=l, pps=pps)
        to_rows = lambda a: jnp.swapaxes(a[:, :n_tok], 0, 1).reshape(n_tok * nb, GROUP_W)
        call_t = jnp.swapaxes(zs["call"].reshape(n_tok, nb, 4 * GROUP_W), 1, 2)
        colv = lambda a: a[l].reshape(-1, 1)
        oc_t, st_s = _hgrn_decode(call_t, colv(lb), colv(loglb), colv(log1m), hgrn_norm_g[l].reshape(-1, 1), st_view[l])
        oc_s = jnp.swapaxes(oc_t, 1, 2).reshape(n_tok * nb, GROUP_W)
        buf_t = jnp.swapaxes(state_ffn_conv[l], 0, 1)
        hs, gate_s = _mixffn(hs, zs["oa"], to_rows(ob_s), oc_s, to_rows(od_s), mods[2], mods[3], mods[4], mods[5],
                             vec(norm_ffn_g), w_out_b[l], w_up_b[l], ffn_conv_w[l], vec(ffn_conv_b), w_dn_b[l],
                             final_norm_g.reshape(1, d), buf_t, prompt=False, seq_len=n_tok, tm=nb, final=final)
        new_s["fox_k"].append(seq_major(zs["bk"]))
        new_s["fox_v"].append(seq_major(zs["bv"]))
        new_s["fox_logf"].append(lf_s[:, :, :N_HEADS])
        new_s["mla_ckv"].append(seq_major(zs["ckv"]))
        new_s["mla_krope"].append(seq_major(zs["krope"])[:, :, :QK_ROPE])
        new_s["hgrn"].append(jnp.transpose(st_s, (3, 0, 1, 2)))
        new_s["ffn_conv"].append(seq_major(gate_s)[:, n_tok - (CONV_W - 1):])
        new_s["gmlp_v"].append(seq_major(zs["av"]))

    y_prompt = hp.reshape(bp, seq_len, d)
    y_sample = seq_major(hs)
    n_pp = tp // page
    stk = lambda lst: jnp.stack(lst)
    return (y_prompt, y_sample,
            stk(new_p["fox_k"]).reshape(depth, n_pp, page, N_HEADS, HEAD_DIM),
            stk(new_p["fox_v"]).reshape(depth, n_pp, page, N_HEADS, HEAD_DIM),
            stk(new_p["fox_logf"]).reshape(depth, n_pp, page, N_HEADS),
            stk(new_p["mla_ckv"]).reshape(depth, n_pp, page, KV_RANK),
            stk(new_p["mla_krope"]).reshape(depth, n_pp, page, QK_ROPE),
            stk(new_p["hgrn"]), stk(new_p["ffn_conv"]),
            stk(new_s["fox_k"]).reshape(depth, nb, n_tok, N_HEADS, HEAD_DIM),
            stk(new_s["fox_v"]).reshape(depth, nb, n_tok, N_HEADS, HEAD_DIM),
            stk(new_s["fox_logf"]),
            stk(new_s["mla_ckv"]), stk(new_s["mla_krope"]),
            stk(new_s["hgrn"]), stk(new_s["ffn_conv"]),
            stk(new_s["gmlp_v"]).reshape(depth, nb, n_tok, N_HEADS, HEAD_DIM))
```

```python
import functools
import math

import jax
import jax.numpy as jnp
from jax import lax
from jax.experimental import pallas as pl
from jax.experimental.pallas import tpu as pltpu

F32 = jnp.float32
BF16 = jnp.bfloat16

N_HEADS = 4
HEAD_DIM = 64
GROUP_W = N_HEADS * HEAD_DIM
CHUNK_A = 128
QK_NOPE = 64
QK_ROPE = 32
KV_RANK = 128
ROPE_THETA = 10000.0
EPS = 1e-6
CONV_W = 3
HGRN_CHUNK = 64
LANES = 128
VMEM_LIMIT = 56 * 1024 * 1024

SEG = dict(a_u=0, a_v=256, b_q=512, b_k=768, b_v=1024, c_all=1280, d_qn=2304, d_qr=2560,
           d_ckv=2688, b_f=2816, d_kr=2944)
N_IN_PACKED = 3072


def _dot(a, b):
    return jnp.dot(a, b, preferred_element_type=F32)


def _dot_nt(a, b):
    return lax.dot_general(a, b, (((1,), (1,)), ((), ())), preferred_element_type=F32)


def _dot_tn(a, b):
    return lax.dot_general(a, b, (((0,), (0,)), ((), ())), preferred_element_type=F32)


def _split3(x):
    x1 = x.astype(BF16)
    r = x - x1.astype(F32)
    x2 = r.astype(BF16)
    x3 = (r - x2.astype(F32)).astype(BF16)
    return x1, x2, x3


def _dot3_left(m, x):
    x1, x2, x3 = _split3(x)
    return _dot(m, x1) + _dot(m, x2) + _dot(m, x3)


def _dot3_right(x, m):
    x1, x2, x3 = _split3(x)
    return _dot(x1, m) + _dot(x2, m) + _dot(x3, m)


def _lane_head(shape):
    return lax.broadcasted_iota(jnp.int32, shape, len(shape) - 1) // HEAD_DIM


def _log_sigmoid(x):
    return -(jnp.maximum(-x, 0.0) + jnp.log1p(jnp.exp(-jnp.abs(x))))


def _sigmoid(x):
    return 1.0 / (1.0 + jnp.exp(-x))


def _silu(x):
    return x * _sigmoid(x)


def _gelu_tanh(x):
    return 0.5 * x * (1.0 + jnp.tanh(math.sqrt(2.0 / math.pi) * (x + 0.044715 * (x * x * x))))


def _rms(x, g):
    return x * lax.rsqrt(jnp.mean(x * x, axis=-1, keepdims=True) + EPS) * g


def _rope_lanes(x, cos, sina, sinb):
    half = QK_ROPE // 2
    return x * cos + pltpu.roll(x, half, 1) * sina + pltpu.roll(x, x.shape[1] - half, 1) * sinb


def _params(sem):
    return pltpu.CompilerParams(dimension_semantics=sem, vmem_limit_bytes=VMEM_LIMIT)


def _lb_kernel(gamma_ref, lb_ref, loglb_ref, log1m_ref):
    g = gamma_ref[...]
    e = jnp.exp(g - jnp.max(g, axis=0, keepdims=True))
    sm = e / jnp.sum(e, axis=0, keepdims=True)
    depth = g.shape[0]
    row = lax.broadcasted_iota(jnp.int32, g.shape, 0)
    cum = jnp.zeros_like(g)
    for u in range(depth):
        cum = cum + jnp.where(row >= u, sm[u:u + 1, :], 0.0)
    lb = cum - cum[0:1, :]
    lb_ref[...] = lb
    loglb_ref[...] = jnp.log(lb)
    log1m_ref[...] = jnp.log1p(-lb)


def _lower_bounds(hgrn_gamma):
    shp = jax.ShapeDtypeStruct(hgrn_gamma.shape, F32)
    return pl.pallas_call(_lb_kernel, out_shape=[shp, shp, shp], name="hgrn_lower_bounds")(hgrn_gamma)


def _ada_kernel(c_ref, w_ref, b_ref, o_ref):
    c = c_ref[...]
    o_ref[...] = _dot(_silu(c).astype(BF16), w_ref[...].astype(BF16)) + b_ref[...]


def _ada_mod(c_all, ada_w, ada_b):
    depth, d, n = ada_w.shape
    rows = c_all.shape[0]
    tn = 1536
    return pl.pallas_call(
        _ada_kernel,
        grid=(depth, n // tn),
        in_specs=[pl.BlockSpec((rows, d), lambda l, j: (0, 0)),
                  pl.BlockSpec((None, d, tn), lambda l, j: (l, 0, j)),
                  pl.BlockSpec((None, 1, tn), lambda l, j: (l, 0, j))],
        out_specs=pl.BlockSpec((None, rows, tn), lambda l, j: (l, 0, j)),
        out_shape=jax.ShapeDtypeStruct((depth, rows, n), F32),
        compiler_params=_params(("arbitrary", "arbitrary")),
        name="ada_modulation",
    )(c_all, ada_w, ada_b.reshape(depth, 1, n))


def _inproj_kernel(*refs, prompt, tiles_per_seq):
    if prompt:
        (x_ref, sh_ref, sc_ref, g_ref, w_ref, wcat_ref, bf_ref, kvg_ref, cos_ref, sina_ref, sinb_ref,
         gw_ref, gb_ref,
         oa_ref, bq_ref, bk_ref, bv_ref, bk16_ref, bv16_ref, logf_ref, cum_ref, call_ref, qcat_ref, kcat_ref,
         ckv_ref, krope_ref, carry_ref) = refs
    else:
        (x_ref, sh_ref, sc_ref, g_ref, w_ref, wcat_ref, bf_ref, kvg_ref, cos_ref, sina_ref, sinb_ref,
         gw_ref, gb_ref,
         oa_ref, av_ref, bq_ref, bk_ref, bv_ref, logf_ref, call_ref, qcat_ref, kcat_ref, ckv_ref,
         krope_ref, vbuf_ref) = refs
    i = pl.program_id(0)
    x = x_ref[...]
    tm = x.shape[0]
    h = _rms(x, g_ref[...]) * (1.0 + sc_ref[...]) + sh_ref[...]
    hb = h.astype(BF16)

    def seg(name, width):
        return _dot(hb, w_ref[:, SEG[name]:SEG[name] + width])

    cos, sina, sinb = cos_ref[...], sina_ref[...], sinb_ref[...]

    a_u = _gelu_tanh(seg("a_u", GROUP_W))
    a_v = _gelu_tanh(seg("a_v", GROUP_W))
    head = _lane_head((CHUNK_A if prompt else tm, GROUP_W))
    if prompt:
        r = lax.broadcasted_iota(jnp.int32, (CHUNK_A, CHUNK_A), 0)
        c = lax.broadcasted_iota(jnp.int32, (CHUNK_A, CHUNK_A), 1)
        for ch in range(tm // CHUNK_A):
            rows = slice(ch * CHUNK_A, (ch + 1) * CHUNK_A)
            vb = a_v[rows].astype(BF16)
            mixed = gb_ref[...]
            for hh in range(N_HEADS):
                wc = jnp.where(r >= c, gw_ref[hh], 0.0).astype(BF16)
                mixed = mixed + jnp.where(head == hh, _dot(wc, vb), 0.0)
            oa_ref[rows, :] = a_u[rows] * mixed
    else:
        @pl.when(i == 0)
        def _():
            vbuf_ref[...] = jnp.zeros_like(vbuf_ref)
        vbuf_ref[i] = a_v
        mixed = jnp.broadcast_to(gb_ref[...], a_v.shape)
        for s in range(vbuf_ref.shape[0]):
            mixed = mixed + gw_ref[s:s + 1, :] * vbuf_ref[s]
        oa_ref[...] = a_u * mixed
        av_ref[...] = a_v

    bq_ref[...] = (seg("b_q", GROUP_W) * (HEAD_DIM ** -0.5)).astype(BF16)
    b_k = seg("b_k", GROUP_W)
    b_v = seg("b_v", GROUP_W)
    bk_ref[...] = b_k
    bv_ref[...] = b_v
    if prompt:
        bk16_ref[...] = b_k.astype(BF16)
        bv16_ref[...] = b_v.astype(BF16)
    lane = lax.broadcasted_iota(jnp.int32, (tm, LANES), 1)
    logf = jnp.where(lane < N_HEADS, _log_sigmoid(seg("b_f", LANES) + bf_ref[...]), 0.0)
    logf_ref[...] = logf
    if prompt:
        @pl.when(i % tiles_per_seq == 0)
        def _():
            carry_ref[...] = jnp.zeros_like(carry_ref)
        rr = lax.broadcasted_iota(jnp.int32, (tm, tm), 0)
        cc = lax.broadcasted_iota(jnp.int32, (tm, tm), 1)
        tri = jnp.where(rr >= cc, 1.0, 0.0).astype(BF16)
        cum = _dot3_left(tri, logf) + carry_ref[...]
        cum_ref[...] = cum
        carry_ref[...] = cum[tm - 1:tm, :]

    call_ref[...] = seg("c_all", 4 * GROUP_W)

    d_qn = seg("d_qn", GROUP_W)
    d_qr = _rope_lanes(seg("d_qr", LANES), cos, sina, sinb)
    qin = jnp.concatenate([d_qn, d_qr], axis=-1).astype(BF16)
    qcat_ref[...] = (_dot(qin, wcat_ref[...]) * ((QK_NOPE + QK_ROPE) ** -0.5)).astype(BF16)
    ckv = _rms(seg("d_ckv", KV_RANK), kvg_ref[...])
    krope = _rope_lanes(seg("d_kr", LANES), cos, sina, sinb)
    ckv_ref[...] = ckv
    krope_ref[...] = krope
    kcat_ref[...] = jnp.concatenate([ckv, krope], axis=-1).astype(BF16)


def _inproj(x, sh, sc, g, w, wcat, bf_pad, kvg, cos, sina, sinb, gw, gb, *, prompt, seq_len, tm):
    t, d = x.shape
    tiles = t // tm
    row = lambda width: pl.BlockSpec((tm, width), lambda i: (i, 0))
    full = lambda a: pl.BlockSpec(a.shape, lambda i: (0,) * a.ndim)
    if prompt:
        tps = seq_len // tm
        mod_spec = pl.BlockSpec((None, 1, d), lambda i: (i // tps, 0, 0))
        tab_spec = pl.BlockSpec((tm, LANES), lambda i: (i % tps, 0))
        gw_spec, gb_spec = full(gw), full(gb)
    else:
        tps = 1
        mod_spec = pl.BlockSpec((tm, d), lambda i: (0, 0))
        tab_spec = pl.BlockSpec((None, 1, LANES), lambda i: (i, 0, 0))
        gw_spec = pl.BlockSpec((None,) + gw.shape[1:], lambda i: (i, 0, 0))
        gb_spec = pl.BlockSpec((None,) + gb.shape[1:], lambda i: (i, 0, 0))
    in_specs = [row(d), mod_spec, mod_spec, full(g), full(w), full(wcat), full(bf_pad), full(kvg),
                tab_spec, tab_spec, tab_spec, gw_spec, gb_spec]
    f = lambda width, dt=F32: jax.ShapeDtypeStruct((t, width), dt)
    outs = [("oa", f(GROUP_W))]
    if not prompt:
        outs.append(("av", f(GROUP_W)))
    outs += [("bq", f(GROUP_W, BF16)), ("bk", f(GROUP_W)), ("bv", f(GROUP_W))]
    if prompt:
        outs += [("bk16", f(GROUP_W, BF16)), ("bv16", f(GROUP_W, BF16))]
    outs.append(("logf", f(LANES)))
    if prompt:
        outs.append(("cum", f(LANES)))
    outs += [("call", f(4 * GROUP_W)), ("qcat", f(N_HEADS * 2 * KV_RANK, BF16)), ("kcat", f(2 * KV_RANK, BF16)),
             ("ckv", f(KV_RANK)), ("krope", f(LANES))]
    scratch = [pltpu.VMEM((1, LANES), F32)] if prompt else [pltpu.VMEM((tiles, tm, GROUP_W), F32)]
    res = pl.pallas_call(
        functools.partial(_inproj_kernel, prompt=prompt, tiles_per_seq=tps),
        grid=(tiles,),
        in_specs=in_specs,
        out_specs=[row(s.shape[1]) for _, s in outs],
        out_shape=[s for _, s in outs],
        scratch_shapes=scratch,
        compiler_params=_params(("arbitrary",)),
        name="inproj_prompt" if prompt else "inproj_sample",
    )(x, sh, sc, g, w, wcat, bf_pad, kvg, cos, sina, sinb, gw, gb)
    return dict(zip([n for n, _ in outs], res))


def _online_softmax_step(s, m_ref, l_ref, acc_ref, pv):
    m_prev = m_ref[...]
    m_new = jnp.maximum(m_prev, jnp.max(s, axis=-1, keepdims=True))
    alpha = jnp.exp(m_prev - m_new)
    p = jnp.exp(s - m_new)
    l_ref[...] = alpha * l_ref[...] + jnp.sum(p, axis=-1, keepdims=True)
    acc_ref[...] = alpha * acc_ref[...] + pv(p.astype(BF16))
    m_ref[...] = m_new


def _init_softmax(m_ref, l_ref, acc_ref):
    m_ref[...] = jnp.full_like(m_ref, -jnp.inf)
    l_ref[...] = jnp.zeros_like(l_ref)
    acc_ref[...] = jnp.zeros_like(acc_ref)


def _head_softmax_update(s, m_ref, l_ref, acc_ref, vb):
    m_prev = m_ref[...]
    m_new = jnp.maximum(m_prev, jnp.max(s, axis=-1, keepdims=True))
    alpha = jnp.exp(m_prev - m_new)
    p = jnp.exp(s - _lane_repeat(m_new, s.shape[1] // LANES))
    l_ref[...] = alpha * l_ref[...] + jnp.sum(p, axis=-1, keepdims=True)
    acc_ref[...] = _lane_repeat(alpha, acc_ref.shape[1] // LANES) * acc_ref[...] + _dot(p.astype(BF16), vb)
    m_ref[...] = m_new


def _lane_repeat(x, n):
    return x if n == 1 else pltpu.repeat(x, n, 1)


def _causal_kv_sweep(i, tq, tk, block):
    n_full = (i * tq) // tk

    def body(j, carry):
        block(j, None)
        return carry

    lax.fori_loop(0, n_full, body, 0)
    qpos = i * tq + lax.broadcasted_iota(jnp.int32, (tq, tk), 0)
    kpos = n_full * tk + lax.broadcasted_iota(jnp.int32, (tq, tk), 1)
    block(n_full, kpos <= qpos)


def _fox_prefill_kernel(q_ref, k_ref, v_ref, cq_ref, ck_ref, o_ref, qst_ref, *scratch, tq, tk):
    i = pl.program_id(1)
    ms, ls, accs = scratch[0:N_HEADS], scratch[N_HEADS:2 * N_HEADS], scratch[2 * N_HEADS:3 * N_HEADS]
    q = q_ref[...]
    head = _lane_head(q.shape)
    for hh in range(N_HEADS):
        _init_softmax(ms[hh], ls[hh], accs[hh])
        qst_ref[hh] = jnp.where(head == hh, q, jnp.zeros_like(q))
    cq = cq_ref[...]

    def block(j, ok):
        start = pl.multiple_of(j * tk, tk)
        kb = k_ref[pl.ds(start, tk), :]
        vb = v_ref[pl.ds(start, tk), :]
        ck = ck_ref[:, pl.ds(start, tk)]
        for hh in range(N_HEADS):
            s = _dot_nt(qst_ref[hh], kb) + (cq[:, hh:hh + 1] - ck[hh:hh + 1, :])
            if ok is not None:
                s = jnp.where(ok, s, -jnp.inf)
            _head_softmax_update(s, ms[hh], ls[hh], accs[hh], vb)

    _causal_kv_sweep(i, tq, tk, block)
    out = jnp.zeros((tq, GROUP_W), F32)
    for hh in range(N_HEADS):
        out = out + jnp.where(head == hh, accs[hh][...] / _lane_repeat(ls[hh][...], GROUP_W // LANES), 0.0)
    o_ref[...] = out


def _fox_prefill(bq, bk16, bv16, cum, cum_t, *, batch, seq_len, tq, tk):
    assert tk % tq == 0
    nq = seq_len // tq
    seq_rows = pl.BlockSpec((seq_len, GROUP_W), lambda b, i: (b, 0))
    stat = [pltpu.VMEM((tq, LANES), F32)] * (2 * N_HEADS)
    return pl.pallas_call(
        functools.partial(_fox_prefill_kernel, tq=tq, tk=tk),
        grid=(batch, nq),
        in_specs=[pl.BlockSpec((tq, GROUP_W), lambda b, i: (b * nq + i, 0)),
                  seq_rows, seq_rows,
                  pl.BlockSpec((tq, LANES), lambda b, i: (b * nq + i, 0)),
                  pl.BlockSpec((None, N_HEADS, seq_len), lambda b, i: (b, 0, 0))],
        out_specs=pl.BlockSpec((tq, GROUP_W), lambda b, i: (b * nq + i, 0)),
        out_shape=jax.ShapeDtypeStruct((batch * seq_len, GROUP_W), F32),
        scratch_shapes=[pltpu.VMEM((N_HEADS, tq, GROUP_W), BF16)] + stat + [pltpu.VMEM((tq, GROUP_W), F32)] * N_HEADS,
        compiler_params=_params(("arbitrary", "arbitrary")),
        name="fox_prefill",
    )(bq, bk16, bv16, cum, cum_t)


def _mla_prefill_kernel(q_ref, kv_ref, wuv_ref, o_ref, *scratch, tq, tk):
    i = pl.program_id(1)
    dq = 2 * KV_RANK
    ms, ls, accs = scratch[0:N_HEADS], scratch[N_HEADS:2 * N_HEADS], scratch[2 * N_HEADS:3 * N_HEADS]
    for hh in range(N_HEADS):
        _init_softmax(ms[hh], ls[hh], accs[hh])

    def block(j, ok):
        kv = kv_ref[pl.ds(pl.multiple_of(j * tk, tk), tk), :]
        vb = kv[:, :KV_RANK]
        for hh in range(N_HEADS):
            s = _dot_nt(q_ref[:, hh * dq:(hh + 1) * dq], kv)
            if ok is not None:
                s = jnp.where(ok, s, -jnp.inf)
            _head_softmax_update(s, ms[hh], ls[hh], accs[hh], vb)

    _causal_kv_sweep(i, tq, tk, block)
    out = jnp.zeros((tq, GROUP_W), F32)
    for hh in range(N_HEADS):
        o_lat = (accs[hh][...] / ls[hh][...]).astype(BF16)
        out = out + _dot(o_lat, wuv_ref[hh])
    o_ref[...] = out


def _mla_prefill(qcat, kcat, wuv_pad, *, batch, seq_len, tq, tk):
    assert tk % tq == 0
    nq = seq_len // tq
    stat = [pltpu.VMEM((tq, LANES), F32)] * (2 * N_HEADS)
    return pl.pallas_call(
        functools.partial(_mla_prefill_kernel, tq=tq, tk=tk),
        grid=(batch, nq),
        in_specs=[pl.BlockSpec((tq, qcat.shape[1]), lambda b, i: (b * nq + i, 0)),
                  pl.BlockSpec((seq_len, kcat.shape[1]), lambda b, i: (b, 0)),
                  pl.BlockSpec(wuv_pad.shape, lambda b, i: (0, 0, 0))],
        out_specs=pl.BlockSpec((tq, GROUP_W), lambda b, i: (b * nq + i, 0)),
        out_shape=jax.ShapeDtypeStruct((batch * seq_len, GROUP_W), F32),
        scratch_shapes=stat + [pltpu.VMEM((tq, KV_RANK), F32)] * N_HEADS,
        compiler_params=_params(("arbitrary", "arbitrary")),
        name="mla_prefill",
    )(qcat, kcat, wuv_pad)


def _hgrn_gates(zf, lb, loglb, log1m):
    a = loglb
    b = log1m + _log_sigmoid(zf)
    log_f = jnp.maximum(a, b) + jnp.log1p(jnp.exp(-jnp.abs(a - b)))
    k_in = (1.0 - lb) * _sigmoid(-zf)
    return log_f, k_in


def _head_rms_rows(o, gain):
    head = _lane_head(o.shape)
    sq = o * o
    ms = jnp.zeros_like(o)
    for hh in range(N_HEADS):
        ms = ms + jnp.where(head == hh, jnp.sum(jnp.where(head == hh, sq, 0.0), axis=-1, keepdims=True), 0.0)
    return o * lax.rsqrt(ms * (1.0 / HEAD_DIM) + EPS) * gain


def _hgrn_prefill_kernel(call_ref, lb_ref, loglb_ref, log1m_ref, gain_ref, o_ref, st_out_ref, st_ref, *, rows, chunk):
    i = pl.program_id(1)

    @pl.when(i == 0)
    def _():
        st_ref[...] = jnp.zeros_like(st_ref)

    w = GROUP_W
    rr = lax.broadcasted_iota(jnp.int32, (chunk, chunk), 0)
    cc = lax.broadcasted_iota(jnp.int32, (chunk, chunk), 1)
    causal = rr >= cc
    tri = jnp.where(causal, 1.0, 0.0).astype(BF16)
    head = _lane_head((chunk, w))
    bd = (lax.broadcasted_iota(jnp.int32, (w, w), 0) // HEAD_DIM) == (lax.broadcasted_iota(jnp.int32, (w, w), 1) // HEAD_DIM)
    half = chunk // 2
    for c in range(rows // chunk):
        rs = slice(c * chunk, (c + 1) * chunk)
        q = call_ref[rs, 0:w]
        zf = call_ref[rs, w:2 * w]
        v = call_ref[rs, 2 * w:3 * w]
        gate = call_ref[rs, 3 * w:4 * w]
        log_f, k_in = _hgrn_gates(zf, lb_ref[...], loglb_ref[...], log1m_ref[...])
        b = _dot3_left(tri, log_f)
        ref = b[half - 1:half, :]
        b_last = b[chunk - 1:chunk, :]
        qt = (q * jnp.exp(b - ref))
        kt = (k_in * jnp.exp(ref - b)).astype(BF16)
        vb = v.astype(BF16)
        st = st_ref[...]
        o = _dot_nt((q * jnp.exp(b)).astype(BF16), st.astype(BF16))
        for hh in range(N_HEADS):
            a = _dot_nt(jnp.where(head == hh, qt, 0.0).astype(BF16), kt)
            a = jnp.where(causal, a, 0.0).astype(BF16)
            o = o + jnp.where(head == hh, _dot(a, vb), 0.0)
        kd = (k_in * jnp.exp(b_last - b)).astype(BF16)
        st_ref[...] = st * jnp.exp(b_last) + jnp.where(bd, _dot_tn(vb, kd), 0.0)
        o_ref[rs, :] = _head_rms_rows(o, gain_ref[...]) * _silu(gate)

    st_out_ref[...] = st_ref[...]


def _hgrn_prefill(call, lb, loglb, log1m, gain, *, batch, seq_len, rows):
    tps = seq_len // rows
    vec = pl.BlockSpec((1, GROUP_W), lambda b, i: (0, 0))
    return pl.pallas_call(
        functools.partial(_hgrn_prefill_kernel, rows=rows, chunk=HGRN_CHUNK),
        grid=(batch, tps),
        in_specs=[pl.BlockSpec((rows, 4 * GROUP_W), lambda b, i: (b * tps + i, 0)), vec, vec, vec, vec],
        out_specs=[pl.BlockSpec((rows, GROUP_W), lambda b, i: (b * tps + i, 0)),
                   pl.BlockSpec((None, GROUP_W, GROUP_W), lambda b, i: (b, 0, 0))],
        out_shape=[jax.ShapeDtypeStruct((batch * seq_len, GROUP_W), F32),
                   jax.ShapeDtypeStruct((batch, GROUP_W, GROUP_W), F32)],
        scratch_shapes=[pltpu.VMEM((GROUP_W, GROUP_W), F32)],
        compiler_params=_params(("arbitrary", "arbitrary")),
        name="hgrn_prefill",
    )(call, lb, loglb, log1m, gain)


def _hgrn_decode_kernel(q_ref, zf_ref, v_ref, gate_ref, lb_ref, loglb_ref, log1m_ref, gain_ref, s_ref,
                        o_ref, s_out_ref, f_scr, k_scr, st_scr):
    n_tok = q_ref.shape[0]
    dk = s_ref.shape[0]
    log_f, k_in = _hgrn_gates(zf_ref[...], lb_ref[...], loglb_ref[...], log1m_ref[...])
    f_scr[...] = jnp.exp(log_f)
    k_scr[...] = k_in
    st_scr[...] = s_ref[...]
    for t in range(n_tok):
        v_t = v_ref[t]

        def body(k, o_acc):
            f_row = f_scr[t, pl.ds(k, 1), :]
            k_row = k_scr[t, pl.ds(k, 1), :]
            q_row = q_ref[t, pl.ds(k, 1), :]
            s_new = st_scr[k] * f_row + k_row * v_t
            st_scr[k] = s_new
            return o_acc + s_new * q_row

        o = lax.fori_loop(0, dk, body, jnp.zeros(v_t.shape, F32))
        o = o * lax.rsqrt(jnp.mean(o * o, axis=0, keepdims=True) + EPS) * gain_ref[...]
        o_ref[t] = o * _silu(gate_ref[t])
    s_out_ref[...] = st_scr[...]


def _hgrn_decode(call_t, lb_c, loglb_c, log1m_c, gain_c, state):
    n_tok, _, nb = call_t.shape
    nh, dk, dv, _ = state.shape
    part = lambda p: pl.BlockSpec((n_tok, HEAD_DIM, nb), lambda h: (0, p * N_HEADS + h, 0))
    col = pl.BlockSpec((HEAD_DIM, 1), lambda h: (h, 0))
    return pl.pallas_call(
        _hgrn_decode_kernel,
        grid=(nh,),
        in_specs=[part(0), part(1), part(2), part(3), col, col, col,
                  pl.BlockSpec((HEAD_DIM, 1), lambda h: (0, 0)),
                  pl.BlockSpec((None, dk, dv, nb), lambda h: (h, 0, 0, 0))],
        out_specs=[pl.BlockSpec((n_tok, HEAD_DIM, nb), lambda h: (0, h, 0)),
                   pl.BlockSpec((None, dk, dv, nb), lambda h: (h, 0, 0, 0))],
        out_shape=[jax.ShapeDtypeStruct((n_tok, GROUP_W, nb), F32),
                   jax.ShapeDtypeStruct(state.shape, F32)],
        scratch_shapes=[pltpu.VMEM((n_tok, HEAD_DIM, nb), F32),
                        pltpu.VMEM((n_tok, HEAD_DIM, nb), F32),
                        pltpu.VMEM((dk, dv, nb), F32)],
        compiler_params=_params(("arbitrary",)),
        name="hgrn_decode",
    )(call_t, call_t, call_t, call_t, lb_c, loglb_c, log1m_c, gain_c, state)


Q_PAD = 8


def _fox_decode_kernel(pt_ref, q_ref, kn_ref, vn_ref, lfn_ref, lfnt_ref, *rest, pps):
    k_refs = rest[0:pps]
    v_refs = rest[pps:2 * pps]
    lf_refs = rest[2 * pps:3 * pps]
    o_ref = rest[3 * pps]
    qbd_ref, kn_pad, vn_pad, m_ref, l_ref, acc_ref, carry_ref, erow_ref = rest[3 * pps + 1:]
    b, c = pl.program_id(0), pl.program_id(1)
    page = kn_pad.shape[0]
    nrow = N_HEADS * Q_PAD
    su = lax.broadcasted_iota(jnp.int32, (page, page), 0)
    sv = lax.broadcasted_iota(jnp.int32, (page, page), 1)
    later = jnp.where(su > sv, 1.0, 0.0).astype(BF16)

    @pl.when((b == 0) & (c == 0))
    def _():
        kn_pad[...] = jnp.zeros_like(kn_pad)
        vn_pad[...] = jnp.zeros_like(vn_pad)

    @pl.when(c == 0)
    def _():
        _init_softmax(m_ref, l_ref, acc_ref)
        q = q_ref[...].astype(F32)
        head = _lane_head(q.shape)
        for hh in range(N_HEADS):
            qbd_ref[hh * Q_PAD:(hh + 1) * Q_PAD, :] = jnp.where(head == hh, q, 0.0)
        kn_pad[0:Q_PAD, :] = kn_ref[...]
        vn_pad[0:Q_PAD, :] = vn_ref[...]
        lfn = lfn_ref[...]
        trow = lax.broadcasted_iota(jnp.int32, lfn.shape, 0)
        e_q = jnp.zeros_like(lfn)
        for u in range(1, Q_PAD):
            e_q = e_q + jnp.where(trow < u, lfn[u:u + 1, :], 0.0)
        for hh in range(N_HEADS):
            erow_ref[hh * Q_PAD:(hh + 1) * Q_PAD, :] = e_q[:, hh:hh + 1]
        lfnt = lfnt_ref[...]
        e_k = _dot3_right(lfnt, later)
        carry_ref[...] = jnp.sum(lfnt, axis=-1, keepdims=True)
        bias = jnp.concatenate([jnp.broadcast_to(e_k[hh:hh + 1, :], (Q_PAD, page)) for hh in range(N_HEADS)], axis=0)
        s = _dot_nt(qbd_ref[...].astype(BF16), kn_pad[...].astype(BF16)) + (bias - erow_ref[...])
        col = lax.broadcasted_iota(jnp.int32, (nrow, page), 1)
        qtok = lax.broadcasted_iota(jnp.int32, (nrow, page), 0) % Q_PAD
        s = jnp.where(col <= qtok, s, -jnp.inf)
        _online_softmax_step(s, m_ref, l_ref, acc_ref, lambda p: _dot(p, vn_pad[...].astype(BF16)))

    qbd = qbd_ref[...].astype(BF16)
    erow = erow_ref[...]
    carry = carry_ref[...]
    parts = []
    for jj in range(pps - 1, -1, -1):
        lf = lf_refs[jj][...]
        e_k = _dot3_right(lf, later) + carry
        carry = carry + jnp.sum(lf, axis=-1, keepdims=True)
        bias = jnp.concatenate([jnp.broadcast_to(e_k[hh:hh + 1, :], (Q_PAD, page)) for hh in range(N_HEADS)], axis=0)
        kt = k_refs[jj][...].reshape(GROUP_W, page).astype(BF16)
        parts.append(_dot(qbd, kt) + (bias - erow))
    carry_ref[...] = carry
    s = jnp.concatenate(parts, axis=-1)

    def pv(p):
        out = jnp.zeros((nrow, GROUP_W), F32)
        for n, jj in enumerate(range(pps - 1, -1, -1)):
            vt = v_refs[jj][...].reshape(GROUP_W, page).astype(BF16)
            out = out + _dot_nt(p[:, n * page:(n + 1) * page], vt)
        return out

    _online_softmax_step(s, m_ref, l_ref, acc_ref, pv)

    @pl.when(c == pl.num_programs(1) - 1)
    def _():
        head = _lane_head((Q_PAD, GROUP_W))
        out = jnp.zeros((Q_PAD, GROUP_W), F32)
        for hh in range(N_HEADS):
            rows = slice(hh * Q_PAD, (hh + 1) * Q_PAD)
            out = out + jnp.where(head == hh, acc_ref[rows, :] / l_ref[rows, :], 0.0)
        o_ref[...] = out


def _fox_decode(page_table, q, kn, vn, lfn, lfnt, cache_k, cache_v, cache_lf, *, layer, pps):
    nb, n_pages = page_table.shape
    page = cache_k.shape[-1]
    n_chunks = n_pages // pps
    seq = lambda a: pl.BlockSpec((None,) + a.shape[1:], lambda b, c, pt: (b,) + (0,) * (a.ndim - 1))

    def paged(a, jj):
        def idx(b, c, pt):
            return (layer, pt[b, (n_chunks - 1 - c) * pps + jj]) + (0,) * (a.ndim - 2)
        return pl.BlockSpec((None, None) + a.shape[2:], idx)

    in_specs = [seq(q), seq(kn), seq(vn), seq(lfn), seq(lfnt)]
    in_specs += [paged(cache_k, jj) for jj in range(pps)]
    in_specs += [paged(cache_v, jj) for jj in range(pps)]
    in_specs += [paged(cache_lf, jj) for jj in range(pps)]
    nrow = N_HEADS * Q_PAD
    return pl.pallas_call(
        functools.partial(_fox_decode_kernel, pps=pps),
        grid_spec=pltpu.PrefetchScalarGridSpec(
            num_scalar_prefetch=1,
            grid=(nb, n_chunks),
            in_specs=in_specs,
            out_specs=pl.BlockSpec((None, Q_PAD, GROUP_W), lambda b, c, pt: (b, 0, 0)),
            scratch_shapes=[pltpu.VMEM((nrow, GROUP_W), F32),
                            pltpu.VMEM((page, GROUP_W), F32),
                            pltpu.VMEM((page, GROUP_W), F32),
                            pltpu.VMEM((nrow, 1), F32),
                            pltpu.VMEM((nrow, 1), F32),
                            pltpu.VMEM((nrow, GROUP_W), F32),
                            pltpu.VMEM((N_HEADS, 1), F32),
                            pltpu.VMEM((nrow, 1), F32)]),
        out_shape=jax.ShapeDtypeStruct((nb, Q_PAD, GROUP_W), F32),
        compiler_params=_params(("arbitrary", "arbitrary")),
        name="fox_decode",
    )(page_table, q, kn, vn, lfn, lfnt, *([cache_k] * pps), *([cache_v] * pps), *([cache_lf] * pps))


def _mla_decode_kernel(pt_ref, q_ref, kn_ref, wuv_ref, *rest, pps):
    ckv_refs = rest[0:pps]
    kr_refs = rest[pps:2 * pps]
    o_ref = rest[2 * pps]
    qst_ref, kn_pad, m_ref, l_ref, acc_ref = rest[2 * pps + 1:]
    b, c = pl.program_id(0), pl.program_id(1)
    page = kn_pad.shape[0]
    nrow = N_HEADS * Q_PAD
    dq = 2 * KV_RANK

    @pl.when((b == 0) & (c == 0))
    def _():
        kn_pad[...] = jnp.zeros_like(kn_pad)

    @pl.when(c == 0)
    def _():
        _init_softmax(m_ref, l_ref, acc_ref)
        for hh in range(N_HEADS):
            qst_ref[hh * Q_PAD:(hh + 1) * Q_PAD, :] = q_ref[:, hh * dq:(hh + 1) * dq].astype(F32)
        kn_pad[0:Q_PAD, :] = kn_ref[...].astype(F32)
        s = _dot_nt(qst_ref[...].astype(BF16), kn_pad[...].astype(BF16))
        col = lax.broadcasted_iota(jnp.int32, (nrow, page), 1)
        qtok = lax.broadcasted_iota(jnp.int32, (nrow, page), 0) % Q_PAD
        s = jnp.where(col <= qtok, s, -jnp.inf)
        _online_softmax_step(s, m_ref, l_ref, acc_ref, lambda p: _dot(p, kn_pad[:, :KV_RANK].astype(BF16)))

    qst = qst_ref[...].astype(BF16)
    q_lat = qst[:, :KV_RANK]
    q_rope = qst[:, KV_RANK:]
    n_rope = kr_refs[0].shape[0]
    zpad = jnp.zeros((KV_RANK - n_rope, page), BF16)
    parts = []
    for jj in range(pps):
        ckv = ckv_refs[jj][...].astype(BF16)
        krt = jnp.concatenate([kr_refs[jj][...].astype(BF16), zpad], axis=0)
        parts.append(_dot_nt(q_lat, ckv) + _dot(q_rope, krt))
    s = jnp.concatenate(parts, axis=-1)

    def pv(p):
        out = jnp.zeros((nrow, KV_RANK), F32)
        for jj in range(pps):
            out = out + _dot(p[:, jj * page:(jj + 1) * page], ckv_refs[jj][...].astype(BF16))
        return out

    _online_softmax_step(s, m_ref, l_ref, acc_ref, pv)

    @pl.when(c == pl.num_programs(1) - 1)
    def _():
        out = jnp.zeros((Q_PAD, GROUP_W), F32)
        for hh in range(N_HEADS):
            rows = slice(hh * Q_PAD, (hh + 1) * Q_PAD)
            o_lat = (acc_ref[rows, :] / l_ref[rows, :]).astype(BF16)
            out = out + _dot(o_lat, wuv_ref[hh])
        o_ref[...] = out


def _mla_decode(page_table, q, kn, wuv_pad, cache_ckv, cache_krt, *, layer, pps):
    nb, n_pages = page_table.shape
    page = cache_ckv.shape[2]
    n_chunks = n_pages // pps
    seq = lambda a: pl.BlockSpec((None,) + a.shape[1:], lambda b, c, pt: (b,) + (0,) * (a.ndim - 1))

    def paged(a, jj):
        def idx(b, c, pt):
            return (layer, pt[b, c * pps + jj], 0, 0)
        return pl.BlockSpec((None, None) + a.shape[2:], idx)

    in_specs = [seq(q), seq(kn), pl.BlockSpec(wuv_pad.shape, lambda b, c, pt: (0, 0, 0))]
    in_specs += [paged(cache_ckv, jj) for jj in range(pps)]
    in_specs += [paged(cache_krt, jj) for jj in range(pps)]
    nrow = N_HEADS * Q_PAD
    return pl.pallas_call(
        functools.partial(_mla_decode_kernel, pps=pps),
        grid_spec=pltpu.PrefetchScalarGridSpec(
            num_scalar_prefetch=1,
            grid=(nb, n_chunks),
            in_specs=in_specs,
            out_specs=pl.BlockSpec((None, Q_PAD, GROUP_W), lambda b, c, pt: (b, 0, 0)),
            scratch_shapes=[pltpu.VMEM((nrow, 2 * KV_RANK), F32),
                            pltpu.VMEM((page, 2 * KV_RANK), F32),
                            pltpu.VMEM((nrow, 1), F32),
                            pltpu.VMEM((nrow, 1), F32),
                            pltpu.VMEM((nrow, KV_RANK), F32)]),
        out_shape=jax.ShapeDtypeStruct((nb, Q_PAD, GROUP_W), F32),
        compiler_params=_params(("arbitrary", "arbitrary")),
        name="mla_decode",
    )(page_table, q, kn, wuv_pad, *([cache_ckv] * pps), *([cache_krt] * pps))


FF_CHUNK = 1408
CARRY_ROWS = 8


def _mixffn_kernel(*refs, prompt, tiles_per_seq, final):
    if prompt:
        (x_ref, oa_ref, ob_ref, oc_ref, od_ref, g1_ref, sh2_ref, sc2_ref, g2_ref, ng_ref, wo_ref, wup_ref,
         cw_ref, cb_ref, wdn_ref, fg_ref, y_ref, nbuf_ref, ext_ref) = refs
    else:
        (x_ref, oa_ref, ob_ref, oc_ref, od_ref, g1_ref, sh2_ref, sc2_ref, g2_ref, ng_ref, wo_ref, wup_ref,
         cw_ref, cb_ref, wdn_ref, fg_ref, buf_ref, y_ref, gate_ref, hist_ref) = refs
    i = pl.program_id(0)
    x = x_ref[...]
    tm, d = x.shape
    d_ff = wdn_ref.shape[0]
    mix = jnp.zeros((tm, d), F32)
    for n, r in enumerate((oa_ref, ob_ref, oc_ref, od_ref)):
        mix = mix + _dot(r[...].astype(BF16), wo_ref[n * GROUP_W:(n + 1) * GROUP_W, :])
    x1 = x + g1_ref[...] * mix
    h2 = (_rms(x1, ng_ref[...]) * (1.0 + sc2_ref[...]) + sh2_ref[...]).astype(BF16)

    if prompt:
        @pl.when(i % tiles_per_seq == 0)
        def _():
            ext_ref[0:CARRY_ROWS, :] = jnp.zeros((CARRY_ROWS, d_ff), F32)
    else:
        @pl.when(i == 0)
        def _():
            hist_ref[...] = buf_ref[...]

    y = jnp.zeros((tm, d), F32)
    for c0 in range(0, d_ff, FF_CHUNK):
        cols = slice(c0, c0 + FF_CHUNK)
        gate = _dot(h2, wup_ref[:, cols])
        up = _dot(h2, wup_ref[:, d_ff + c0:d_ff + c0 + FF_CHUNK])
        if prompt:
            ext_ref[CARRY_ROWS:CARRY_ROWS + tm, cols] = gate
            prev1 = ext_ref[pl.ds(CARRY_ROWS - 1, tm), cols]
            prev2 = ext_ref[pl.ds(CARRY_ROWS - 2, tm), cols]
        else:
            prev2 = hist_ref[0, :, cols]
            prev1 = hist_ref[1, :, cols]
            hist_ref[0, :, cols] = prev1
            hist_ref[1, :, cols] = gate
            gate_ref[:, cols] = gate
        conv = cb_ref[:, cols] + prev2 * cw_ref[0:1, cols] + prev1 * cw_ref[1:2, cols] + gate * cw_ref[2:3, cols]
        act = (_silu(conv) * up).astype(BF16)
        y = y + _dot(act, wdn_ref[cols, :])
    if prompt:
        tail = ext_ref[tm:tm + CARRY_ROWS, :]
        ext_ref[0:CARRY_ROWS, :] = tail
        nbuf_ref[...] = tail[CARRY_ROWS - (CONV_W - 1):, :]
    x2 = x1 + g2_ref[...] * y
    if final:
        x2 = _rms(x2, fg_ref[...])
    y_ref[...] = x2


def _mixffn(x, oa, ob, oc, od, g1, sh2, sc2, g2, ng, wo, wup, cw, cb, wdn, fg, buf, *, prompt, seq_len, tm, final):
    t, d = x.shape
    d_ff = wdn.shape[0]
    tiles = t // tm
    row = lambda width: pl.BlockSpec((tm, width), lambda i: (i, 0))
    full = lambda a: pl.BlockSpec(a.shape, lambda i: (0,) * a.ndim)
    if prompt:
        tps = seq_len // tm
        mod_spec = pl.BlockSpec((None, 1, d), lambda i: (i // tps, 0, 0))
    else:
        tps = 1
        mod_spec = pl.BlockSpec((tm, d), lambda i: (0, 0))
    in_specs = [row(d), row(GROUP_W), row(GROUP_W), row(GROUP_W), row(GROUP_W), mod_spec, mod_spec, mod_spec,
                mod_spec, full(ng), full(wo), full(wup), full(cw), full(cb), full(wdn), full(fg)]
    args = [x, oa, ob, oc, od, g1, sh2, sc2, g2, ng, wo, wup, cw, cb, wdn, fg]
    if prompt:
        out_specs = [row(d), pl.BlockSpec((None, CONV_W - 1, d_ff), lambda i: (i // tps, 0, 0))]
        out_shape = [jax.ShapeDtypeStruct((t, d), F32), jax.ShapeDtypeStruct((t // seq_len, CONV_W - 1, d_ff), F32)]
        scratch = [pltpu.VMEM((CARRY_ROWS + tm, d_ff), F32)]
    else:
        in_specs.append(full(buf))
        args.append(buf)
        out_specs = [row(d), row(d_ff)]
        out_shape = [jax.ShapeDtypeStruct((t, d), F32), jax.ShapeDtypeStruct((t, d_ff), F32)]
        scratch = [pltpu.VMEM((CONV_W - 1, tm, d_ff), F32)]
    return pl.pallas_call(
        functools.partial(_mixffn_kernel, prompt=prompt, tiles_per_seq=tps, final=final),
        grid=(tiles,),
        in_specs=in_specs,
        out_specs=out_specs,
        out_shape=out_shape,
        scratch_shapes=scratch,
        compiler_params=_params(("arbitrary",)),
        name="mixffn_prompt" if prompt else "mixffn_sample",
    )(*args)


def _rope_tables(pos):
    half = QK_ROPE // 2
    inv_freq = ROPE_THETA ** (-jnp.arange(half, dtype=F32) / half)
    ang = pos.astype(F32)[:, None] * inv_freq[None, :]
    cos, sin, zero = jnp.cos(ang), jnp.sin(ang), jnp.zeros_like(ang)
    reps = LANES // QK_ROPE
    tile = lambda lo, hi: jnp.tile(jnp.concatenate([lo, hi], axis=-1), (1, reps))
    return tile(cos, cos), tile(zero, sin), tile(-sin, zero)


def _pack_w_in(w_in):
    sizes = (256, 256, 256, 256, 256, 4, 256, 256, 256, 256, 256, 128, 128, 32)
    offs = [0]
    for s in sizes:
        offs.append(offs[-1] + s)
    col = lambda n: w_in[:, :, offs[n]:offs[n + 1]]
    pad = lambda a: jnp.pad(a, ((0, 0), (0, 0), (0, LANES - a.shape[2])))
    parts = [col(0), col(1), col(2), col(3), col(4), col(6), col(7), col(8), col(9), col(10), col(11), col(12),
             pad(col(5)), pad(col(13))]
    return jnp.concatenate(parts, axis=-1).astype(BF16)


def _pack_w_qcat(w_uk):
    r = w_uk.shape[0]
    w = jnp.zeros((N_HEADS * QK_NOPE + N_HEADS * QK_ROPE, N_HEADS * 2 * r), F32)
    eye = jnp.eye(QK_ROPE, dtype=F32)
    for hh in range(N_HEADS):
        w = w.at[hh * QK_NOPE:(hh + 1) * QK_NOPE, hh * 2 * r:hh * 2 * r + r].set(w_uk[:, hh, :].T)
        w = w.at[N_HEADS * QK_NOPE + hh * QK_ROPE:N_HEADS * QK_NOPE + (hh + 1) * QK_ROPE,
                 hh * 2 * r + r:hh * 2 * r + r + QK_ROPE].set(eye)
    return w.astype(BF16)


def _pack_w_uv(w_uv):
    r = w_uv.shape[0]
    w = jnp.zeros((N_HEADS, r, GROUP_W), F32)
    for hh in range(N_HEADS):
        w = w.at[hh, :, hh * HEAD_DIM:(hh + 1) * HEAD_DIM].set(w_uv[:, hh, :])
    return w.astype(BF16)


def _pick(n, candidates):
    for c in candidates:
        if n % c == 0:
            return c
    return n


def kernel(x_prompt, x_sample, c_prompt, c_sample, cache_fox_k, cache_fox_v, cache_fox_logf, cache_mla_ckv, cache_mla_krope, state_hgrn, state_ffn_conv, page_table, ada_w, ada_b, norm_attn_g, norm_ffn_g, w_in, gmlp_ws, gmlp_b, fox_bf, hgrn_gamma, hgrn_norm_g, mla_kv_norm_g, mla_w_uk, mla_w_uv, w_out, ffn_w_up, ffn_conv_w, ffn_conv_b, ffn_w_down, final_norm_g):
    depth = w_in.shape[0]
    bp, seq_len, d = x_prompt.shape
    nb, n_tok, _ = x_sample.shape
    n_pages = page_table.shape[1]
    page = cache_fox_k.shape[2]
    d_ff = ffn_w_down.shape[1]
    past_len = n_pages * page
    tp = bp * seq_len

    tm_p = _pick(seq_len, (256, 128))
    tq = _pick(seq_len, (256, 128))
    tk = _pick(seq_len, (512, 256, 128))
    hg_rows = _pick(seq_len, (256, 128, 64))
    pps = _pick(n_pages, (64, 32, 16, 8, 4, 2, 1))

    w_in_p = _pack_w_in(w_in)
    w_out_b = w_out.astype(BF16)
    w_up_b = ffn_w_up.astype(BF16)
    w_dn_b = ffn_w_down.astype(BF16)
    bf_pad = jnp.pad(fox_bf, ((0, 0), (0, LANES - fox_bf.shape[1])))
    gain_row = jnp.tile(hgrn_norm_g, (1, N_HEADS))
    lb, loglb, log1m = _lower_bounds(hgrn_gamma.astype(F32))
    cos_p, sina_p, sinb_p = _rope_tables(jnp.arange(seq_len))
    cos_s, sina_s, sinb_s = (a.reshape(n_tok, 1, LANES) for a in _rope_tables(past_len + jnp.arange(n_tok)))
    gb_p = jnp.repeat(jnp.swapaxes(gmlp_b, 1, 2), HEAD_DIM, axis=2)
    gw_s = jnp.repeat(jnp.transpose(gmlp_ws[:, :, :n_tok, :n_tok], (0, 2, 3, 1)), HEAD_DIM, axis=3)
    gw_s = jnp.where((jnp.arange(n_tok)[:, None] >= jnp.arange(n_tok)[None, :])[None, :, :, None], gw_s, 0.0)
    gb_s = gb_p[:, :n_tok].reshape(depth, n_tok, 1, GROUP_W)

    rows_c = bp + nb
    rows_pad = -(-rows_c // 8) * 8
    c_all = jnp.pad(jnp.concatenate([c_prompt, c_sample], axis=0), ((0, rows_pad - rows_c), (0, 0)))
    mod = _ada_mod(c_all, ada_w, ada_b)

    ck_view = jnp.transpose(cache_fox_k, (0, 1, 3, 4, 2))
    cv_view = jnp.transpose(cache_fox_v, (0, 1, 3, 4, 2))
    clf_view = jnp.transpose(cache_fox_logf, (0, 1, 3, 2))
    ckr_view = jnp.transpose(cache_mla_krope, (0, 1, 3, 2))
    st_view = jnp.transpose(state_hgrn, (0, 2, 3, 4, 1))

    hp = x_prompt.reshape(tp, d)
    hs = jnp.swapaxes(x_sample, 0, 1).reshape(n_tok * nb, d)
    new_p = {k: [] for k in ("fox_k", "fox_v", "fox_logf", "mla_ckv", "mla_krope", "hgrn", "ffn_conv")}
    new_s = {k: [] for k in ("fox_k", "fox_v", "fox_logf", "mla_ckv", "mla_krope", "hgrn", "ffn_conv", "gmlp_v")}

    def seq_major(a):
        return jnp.swapaxes(a.reshape(n_tok, nb, a.shape[-1]), 0, 1)

    def pad_q(a):
        return jnp.pad(a, ((0, 0), (0, Q_PAD - n_tok), (0, 0)))

    for l in range(depth):
        m6 = mod[l].reshape(rows_pad, 6, d)
        modp = [m6[:bp, n].reshape(bp, 1, d) for n in range(6)]
        mods = [m6[bp:bp + nb, n] for n in range(6)]
        wcat = _pack_w_qcat(mla_w_uk[l])
        wuv_pad = _pack_w_uv(mla_w_uv[l])
        vec = lambda a: a[l].reshape(1, -1)
        final = l == depth - 1

        zp = _inproj(hp, modp[0], modp[1], vec(norm_attn_g), w_in_p[l], wcat, vec(bf_pad), vec(mla_kv_norm_g),
                     cos_p, sina_p, sinb_p, gmlp_ws[l], gb_p[l], prompt=True, seq_len=seq_len, tm=tm_p)
        cum_t = jnp.swapaxes(zp["cum"][:, :N_HEADS].reshape(bp, seq_len, N_HEADS), 1, 2)
        ob = _fox_prefill(zp["bq"], zp["bk16"], zp["bv16"], zp["cum"], cum_t, batch=bp, seq_len=seq_len, tq=tq, tk=tk)
        oc, st_p = _hgrn_prefill(zp["call"], vec(lb), vec(loglb), vec(log1m), vec(gain_row),
                                 batch=bp, seq_len=seq_len, rows=hg_rows)
        od = _mla_prefill(zp["qcat"], zp["kcat"], wuv_pad, batch=bp, seq_len=seq_len, tq=tq, tk=tk)
        hp, nbuf_p = _mixffn(hp, zp["oa"], ob, oc, od, modp[2], modp[3], modp[4], modp[5], vec(norm_ffn_g),
                             w_out_b[l], w_up_b[l], ffn_conv_w[l], vec(ffn_conv_b), w_dn_b[l],
                             final_norm_g.reshape(1, d), None, prompt=True, seq_len=seq_len, tm=tm_p, final=final)
        new_p["fox_k"].append(zp["bk"])
        new_p["fox_v"].append(zp["bv"])
        new_p["fox_logf"].append(zp["logf"][:, :N_HEADS])
        new_p["mla_ckv"].append(zp["ckv"])
        new_p["mla_krope"].append(zp["krope"][:, :QK_ROPE])
        st4 = st_p.reshape(bp, N_HEADS, HEAD_DIM, N_HEADS, HEAD_DIM)
        new_p["hgrn"].append(jnp.stack([jnp.swapaxes(st4[:, hh, :, hh, :], 1, 2) for hh in range(N_HEADS)], axis=1))
        new_p["ffn_conv"].append(nbuf_p)

        zs = _inproj(hs, mods[0], mods[1], vec(norm_attn_g), w_in_p[l], wcat, vec(bf_pad), vec(mla_kv_norm_g),
                     cos_s, sina_s, sinb_s, gw_s[l], gb_s[l], prompt=False, seq_len=n_tok, tm=nb)
        q_s, kn_s, vn_s = (pad_q(seq_major(zs[n])) for n in ("bq", "bk", "bv"))
        lf_s = seq_major(zs["logf"])
        lfn = pad_q(lf_s)
        lfnt = jnp.pad(jnp.swapaxes(lf_s[:, :, :N_HEADS], 1, 2), ((0, 0), (0, 0), (0, page - n_tok)))
        ob_s = _fox_decode(page_table, q_s, kn_s, vn_s, lfn, lfnt, ck_view, cv_view, clf_view, layer=l, pps=pps)
        od_s = _mla_decode(page_table, pad_q(seq_major(zs["qcat"])), pad_q(seq_major(zs["kcat"])), wuv_pad,
                           cache_mla_ckv, ckr_view, layer=l, pps=pps)
        to_rows = lambda a: jnp.swapaxes(a[:, :n_tok], 0, 1).reshape(n_tok * nb, GROUP_W)
        call_t = jnp.swapaxes(zs["call"].reshape(n_tok, nb, 4 * GROUP_W), 1, 2)
        colv = lambda a: a[l].reshape(-1, 1)
        oc_t, st_s = _hgrn_decode(call_t, colv(lb), colv(loglb), colv(log1m), hgrn_norm_g[l].reshape(-1, 1), st_view[l])
        oc_s = jnp.swapaxes(oc_t, 1, 2).reshape(n_tok * nb, GROUP_W)
        buf_t = jnp.swapaxes(state_ffn_conv[l], 0, 1)
        hs, gate_s = _mixffn(hs, zs["oa"], to_rows(ob_s), oc_s, to_rows(od_s), mods[2], mods[3], mods[4], mods[5],
                             vec(norm_ffn_g), w_out_b[l], w_up_b[l], ffn_conv_w[l], vec(ffn_conv_b), w_dn_b[l],
                             final_norm_g.reshape(1, d), buf_t, prompt=False, seq_len=n_tok, tm=nb, final=final)
        new_s["fox_k"].append(seq_major(zs["bk"]))
        new_s["fox_v"].append(seq_major(zs["bv"]))
        new_s["fox_logf"].append(lf_s[:, :, :N_HEADS])
        new_s["mla_ckv"].append(seq_major(zs["ckv"]))
        new_s["mla_krope"].append(seq_major(zs["krope"])[:, :, :QK_ROPE])
        new_s["hgrn"].append(jnp.transpose(st_s, (3, 0, 1, 2)))
        new_s["ffn_conv"].append(seq_major(gate_s)[:, n_tok - (CONV_W - 1):])
        new_s["gmlp_v"].append(seq_major(zs["av"]))

    y_prompt = hp.reshape(bp, seq_len, d)
    y_sample = seq_major(hs)
    n_pp = tp // page
    stk = lambda lst: jnp.stack(lst)
    return (y_prompt, y_sample,
            stk(new_p["fox_k"]).reshape(depth, n_pp, page, N_HEADS, HEAD_DIM),
            stk(new_p["fox_v"]).reshape(depth, n_pp, page, N_HEADS, HEAD_DIM),
            stk(new_p["fox_logf"]).reshape(depth, n_pp, page, N_HEADS),
            stk(new_p["mla_ckv"]).reshape(depth, n_pp, page, KV_RANK),
            stk(new_p["mla_krope"]).reshape(depth, n_pp, page, QK_ROPE),
            stk(new_p["hgrn"]), stk(new_p["ffn_conv"]),
            stk(new_s["fox_k"]).reshape(depth, nb, n_tok, N_HEADS, HEAD_DIM),
            stk(new_s["fox_v"]).reshape(depth, nb, n_tok, N_HEADS, HEAD_DIM),
            stk(new_s["fox_logf"]),
            stk(new_s["mla_ckv"]), stk(new_s["mla_krope"]),
            stk(new_s["hgrn"]), stk(new_s["ffn_conv"]),
            stk(new_s["gmlp_v"]).reshape(depth, nb, n_tok, N_HEADS, HEAD_DIM))
```
